```python
import numpy as np
import jax
import jax.numpy as jnp
from jax import lax

D_MODEL = 1024
BATCH = 16
SEQ = 2048
DEPTH = 2

HEAD_DIM = 64
ROPE_THETA = 10000.0
NORM_EPS = 1e-6
NEG = -1e30
BRANCH_W = D_MODEL // 2
N_BRANCH = 4
NSA_HEADS = BRANCH_W // HEAD_DIM
NSA_KV = 2
NSA_R = NSA_HEADS // NSA_KV
CMP_LEN = 32
CMP_STRIDE = 16
CMP_HIDDEN = 256
SEL_LEN = 64
SEL_TOPK = 16
NSA_WINDOW = 512
SEL_Q_CHUNK = 32
FORCE_BONUS = 1e3
SWA_HEADS = BRANCH_W // HEAD_DIM
SWA_KV = 2
SWA_R = SWA_HEADS // SWA_KV
SWA_WINDOW = 128
Q_BLOCK = 128
CONV_CH = BRANCH_W
CONV_WIDTH = 31
POOL_CH = BRANCH_W
POOL_WINDOWS = (2, 4, 8, 16)
POOL_GROUPS = 4
POOL_GROUP_CH = POOL_CH // POOL_GROUPS
D_FF = ((8 * D_MODEL // 3) + 127) // 128 * 128
FFN_CONV_WIDTH = 3
IN_SIZES = (
    NSA_HEADS * HEAD_DIM,
    3 * 2 * NSA_KV * HEAD_DIM,
    3 * NSA_HEADS,
    SWA_HEADS * HEAD_DIM,
    2 * SWA_KV * HEAD_DIM,
    2 * CONV_CH,
    POOL_CH,
)
IN_COLS = sum(IN_SIZES)

kernel_name = "hybrid_nsa_conformer_pool_swa_block"


def rms_norm(x, g):
    xf = x.astype(jnp.float32)
    y = xf * lax.rsqrt(jnp.mean(xf * xf, axis=-1, keepdims=True) + NORM_EPS)
    return (y * g.astype(jnp.float32)).astype(x.dtype)


def layer_norm(x, g, b):
    xf = x.astype(jnp.float32)
    mu = jnp.mean(xf, axis=-1, keepdims=True)
    var = jnp.mean(jnp.square(xf - mu), axis=-1, keepdims=True)
    y = (xf - mu) * lax.rsqrt(var + NORM_EPS)
    return (y * g.astype(jnp.float32) + b.astype(jnp.float32)).astype(x.dtype)


def rope_tables(seq):
    inv = 1.0 / (ROPE_THETA ** (jnp.arange(0, HEAD_DIM, 2, dtype=jnp.float32) / HEAD_DIM))
    ang = jnp.arange(seq, dtype=jnp.float32)[:, None] * inv[None, :]
    return jnp.cos(ang), jnp.sin(ang)


def apply_rope(x, cos, sin):
    shp = (1, x.shape[1]) + (1,) * (x.ndim - 3) + (HEAD_DIM // 2,)
    c, s = cos.reshape(shp), sin.reshape(shp)
    xf = x.astype(jnp.float32)
    x1, x2 = xf[..., : HEAD_DIM // 2], xf[..., HEAD_DIM // 2:]
    return jnp.concatenate([x1 * c - x2 * s, x2 * c + x1 * s], axis=-1).astype(x.dtype)


def causal_depthwise_conv(x, w, b):
    k, c = w.shape
    y = lax.conv_general_dilated(
        x, w[:, None, :].astype(x.dtype), window_strides=(1,), padding=[(k - 1, 0)],
        dimension_numbers=("NWC", "WIO", "NWC"), feature_group_count=c)
    return y + b.astype(x.dtype)


def masked_softmax(s, mask):
    p = jax.nn.softmax(jnp.where(mask, s, NEG), axis=-1)
    return jnp.where(mask, p, 0.0)


def banded_attention(q, k, v, window, sinks):
    bsz, seq, g, r, dk = q.shape
    span = window + Q_BLOCK
    k_pad = jnp.pad(k, ((0, 0), (window, 0), (0, 0), (0, 0)))
    v_pad = jnp.pad(v, ((0, 0), (window, 0), (0, 0), (0, 0)))
    scale = dk ** -0.5

    def block(i):
        start = i * Q_BLOCK
        qb = lax.dynamic_slice_in_dim(q, start, Q_BLOCK, axis=1)
        kb = lax.dynamic_slice_in_dim(k_pad, start, span, axis=1)
        vb = lax.dynamic_slice_in_dim(v_pad, start, span, axis=1)
        t = start + jnp.arange(Q_BLOCK)
        j = start - window + jnp.arange(span)
        mask = (j[None, :] <= t[:, None]) & (j[None, :] > t[:, None] - window) & (j[None, :] >= 0)
        mask = mask[None, None, None]
        s = jnp.einsum("bqgrd,bkgd->bgrqk", qb, kb, preferred_element_type=jnp.float32) * scale
        if sinks is None:
            p = masked_softmax(s, mask)
        else:
            sink = sinks.astype(jnp.float32).reshape(g, r)[None, :, :, None, None]
            s = jnp.where(mask, s, NEG)
            m = jnp.maximum(jnp.max(s, axis=-1, keepdims=True), sink)
            e = jnp.exp(s - m)
            p = e / (jnp.sum(e, axis=-1, keepdims=True) + jnp.exp(sink - m))
        return jnp.einsum("bgrqk,bkgd->bqgrd", p.astype(v.dtype), vb)

    out = lax.map(block, jnp.arange(seq // Q_BLOCK))
    return jnp.moveaxis(out, 0, 1).reshape(bsz, seq, g, r, dk)


def nsa_compress(k, pos, w1, w2):
    bsz, seq, g, dk = k.shape
    n_cmp = (seq - CMP_LEN) // CMP_STRIDE + 1
    idx = np.arange(n_cmp)[:, None] * CMP_STRIDE + np.arange(CMP_LEN)[None, :]
    blk = jnp.swapaxes(k[:, idx], 2, 3) + pos.astype(k.dtype)
    flat = blk.reshape(bsz, n_cmp, g, CMP_LEN * dk)
    return jax.nn.gelu(flat @ w1, approximate=True) @ w2


def nsa_compressed_attention(q, kc, vc):
    seq, n_cmp = q.shape[1], kc.shape[1]
    t = jnp.arange(seq)
    blk_end = jnp.arange(n_cmp) * CMP_STRIDE + CMP_LEN - 1
    mask = (blk_end[None, :] <= t[:, None])[None, :, None, None, :]
    s = jnp.einsum("bsgrd,bngd->bsgrn", q, kc, preferred_element_type=jnp.float32) * HEAD_DIM ** -0.5
    p = masked_softmax(s, mask)
    o = jnp.einsum("bsgrn,bngd->bsgrd", p.astype(vc.dtype), vc)
    return o, p


def nsa_select_blocks(p_cmp, seq):
    n_cmp = p_cmp.shape[-1]
    n_slc = seq // SEL_LEN
    i = jnp.arange(n_cmp)
    j = jnp.arange(n_slc)
    overlap = ((i[:, None] * CMP_STRIDE < (j[None, :] + 1) * SEL_LEN)
               & (i[:, None] * CMP_STRIDE + CMP_LEN > j[None, :] * SEL_LEN)).astype(jnp.float32)
    imp = jnp.einsum("bsgrn,nj->bsgj", p_cmp, overlap)
    cur = jnp.arange(seq) // SEL_LEN
    valid = (j[None, :] <= cur[:, None])[None, :, None, :]
    forced = ((j[None, :] == 0) | (j[None, :] == cur[:, None]) | (j[None, :] == cur[:, None] - 1))[None, :, None, :]
    score = jnp.where(valid, imp + jnp.where(forced, FORCE_BONUS, 0.0), -1.0)
    _, sel = lax.top_k(score, min(SEL_TOPK, n_slc))
    return sel


def nsa_selected_attention(q, k, v, sel):
    bsz, seq, g, r, dk = q.shape
    n_slc = seq // SEL_LEN
    n_top = sel.shape[-1]
    kb = k.reshape(bsz, n_slc, SEL_LEN, g, dk).transpose(0, 3, 1, 2, 4)
    vb = v.reshape(bsz, n_slc, SEL_LEN, g, dk).transpose(0, 3, 1, 2, 4)
    nc = seq // SEL_Q_CHUNK
    qc = jnp.moveaxis(q.reshape(bsz, nc, SEL_Q_CHUNK, g, r, dk), 1, 0)
    sc = jnp.moveaxis(sel.reshape(bsz, nc, SEL_Q_CHUNK, g, n_top), 1, 0)
    b_idx = jnp.arange(bsz)[:, None, None, None]
    g_idx = jnp.arange(g)[None, None, :, None]
    scale = dk ** -0.5

    def chunk(args):
        q_c, sel_c, c = args
        t = c * SEL_Q_CHUNK + jnp.arange(SEL_Q_CHUNK)
        kg = kb[b_idx, g_idx, sel_c]
        vg = vb[b_idx, g_idx, sel_c]
        s = jnp.einsum("bcgrd,bcgkld->bcgrkl", q_c, kg, preferred_element_type=jnp.float32) * scale
        kpos = sel_c[..., None] * SEL_LEN + jnp.arange(SEL_LEN)
        mask = (kpos <= t[None, :, None, None, None])[:, :, :, None]
        flat = (bsz, SEL_Q_CHUNK, g, r, n_top * SEL_LEN)
        p = masked_softmax(s.reshape(flat), mask.reshape(bsz, SEL_Q_CHUNK, g, 1, n_top * SEL_LEN)).reshape(s.shape)
        return jnp.einsum("bcgrkl,bcgkld->bcgrd", p.astype(vg.dtype), vg)

    out = lax.map(chunk, (qc, sc, jnp.arange(nc)))
    return jnp.moveaxis(out, 0, 1).reshape(bsz, seq, g, r, dk)


def conformer_conv(u, conv_w, conv_b, ln_g, ln_b):
    a, gte = jnp.split(u, 2, axis=-1)
    c = a * jax.nn.sigmoid(gte)
    c = causal_depthwise_conv(c, conv_w, conv_b)
    return jax.nn.silu(layer_norm(c, ln_g, ln_b))


def pool_mixer(u, w, scale):
    bsz, seq, _ = u.shape
    uf = u.reshape(bsz, seq, POOL_GROUPS, POOL_GROUP_CH).astype(jnp.float32)
    cs = jnp.pad(jnp.cumsum(uf, axis=1), ((0, 0), (1, 0), (0, 0), (0, 0)))
    t = jnp.arange(seq)[:, None]
    win = jnp.array(POOL_WINDOWS, dtype=jnp.int32)[None, :]
    lo = jnp.maximum(t + 1 - win, 0)
    g_idx = jnp.arange(POOL_GROUPS)[None, :]
    mean = (cs[:, 1:] - cs[:, lo, g_idx]) / (t + 1 - lo).astype(jnp.float32)[None, :, :, None]
    pooled = (mean - uf).astype(u.dtype)
    mixed = jnp.einsum("bsgc,gcd->bsgd", pooled, w)
    return mixed.reshape(bsz, seq, POOL_CH) * scale


def setup_inputs(seed: int = 0) -> dict:
    key = jax.random.key(seed)
    ks = jax.random.split(key, 24)
    f32 = jnp.float32
    L = DEPTH

    def nrm(k, shape, scale):
        return jax.random.normal(k, shape, f32) * scale

    return {
        "x": nrm(ks[0], (BATCH, SEQ, D_MODEL), 1.0),
        "norm_mix_pre": 1.0 + nrm(ks[1], (L, D_MODEL), 0.05),
        "norm_mix_post": 1.0 + nrm(ks[2], (L, D_MODEL), 0.05),
        "norm_ffn_pre": 1.0 + nrm(ks[3], (L, D_MODEL), 0.05),
        "norm_ffn_post": 1.0 + nrm(ks[4], (L, D_MODEL), 0.05),
        "w_in": nrm(ks[5], (L, D_MODEL, IN_COLS), D_MODEL ** -0.5),
        "nsa_cmp_pos": nrm(ks[6], (L, 2, CMP_LEN, HEAD_DIM), 0.1),
        "nsa_cmp_w1": nrm(ks[7], (L, 2, CMP_LEN * HEAD_DIM, CMP_HIDDEN), (CMP_LEN * HEAD_DIM) ** -0.5),
        "nsa_cmp_w2": nrm(ks[8], (L, 2, CMP_HIDDEN, HEAD_DIM), CMP_HIDDEN ** -0.5),
        "swa_sinks": nrm(ks[9], (L, SWA_HEADS), 0.5),
        "conv_w": nrm(ks[10], (L, CONV_WIDTH, CONV_CH), CONV_WIDTH ** -0.5),
        "conv_b": nrm(ks[11], (L, CONV_CH), 0.01),
        "conv_ln_g": 1.0 + nrm(ks[12], (L, CONV_CH), 0.05),
        "conv_ln_b": nrm(ks[13], (L, CONV_CH), 0.01),
        "pool_w": nrm(ks[14], (L, POOL_GROUPS, POOL_GROUP_CH, POOL_GROUP_CH), POOL_GROUP_CH ** -0.5),
        "pool_scale": 1.0 + nrm(ks[15], (L, POOL_CH), 0.1),
        "w_branch": nrm(ks[16], (L, N_BRANCH, BRANCH_W, D_MODEL), BRANCH_W ** -0.5),
        "w_gate": nrm(ks[17], (L, D_MODEL, N_BRANCH * D_MODEL), D_MODEL ** -0.5),
        "w_o": nrm(ks[18], (L, D_MODEL, D_MODEL), D_MODEL ** -0.5),
        "ffn_w_up": nrm(ks[19], (L, D_MODEL, 2 * D_FF), D_MODEL ** -0.5),
        "ffn_conv_w": nrm(ks[20], (L, FFN_CONV_WIDTH, 2 * D_FF), FFN_CONV_WIDTH ** -0.5),
        "ffn_conv_b": nrm(ks[21], (L, 2 * D_FF), 0.01),
        "ffn_w_down": nrm(ks[22], (L, D_FF, D_MODEL), D_FF ** -0.5),
    }


def reference(x, norm_mix_pre, norm_mix_post, norm_ffn_pre, norm_ffn_post, w_in,
              nsa_cmp_pos, nsa_cmp_w1, nsa_cmp_w2, swa_sinks, conv_w, conv_b, conv_ln_g,
              conv_ln_b, pool_w, pool_scale, w_branch, w_gate, w_o, ffn_w_up, ffn_conv_w,
              ffn_conv_b, ffn_w_down):
    bsz, seq, _ = x.shape
    cos, sin = rope_tables(seq)
    splits = np.cumsum(IN_SIZES)[:-1].tolist()
    for l in range(DEPTH):
        h = rms_norm(x, norm_mix_pre[l])
        proj = h @ w_in[l]
        a_q, a_kv, a_gate, d_q, d_kv, b_in, c_in = jnp.split(proj, splits, axis=-1)

        qa = a_q.reshape(bsz, seq, NSA_KV, NSA_R, HEAD_DIM)
        kv = a_kv.reshape(bsz, seq, 3, 2, NSA_KV, HEAD_DIM)
        k_cmp = nsa_compress(kv[:, :, 0, 0], nsa_cmp_pos[l, 0], nsa_cmp_w1[l, 0], nsa_cmp_w2[l, 0])
        v_cmp = nsa_compress(kv[:, :, 0, 1], nsa_cmp_pos[l, 1], nsa_cmp_w1[l, 1], nsa_cmp_w2[l, 1])
        o_cmp, p_cmp = nsa_compressed_attention(qa, k_cmp, v_cmp)
        sel = nsa_select_blocks(p_cmp, seq)
        qa_rot = apply_rope(qa, cos, sin)
        o_slc = nsa_selected_attention(qa_rot, apply_rope(kv[:, :, 1, 0], cos, sin), kv[:, :, 1, 1], sel)
        o_win = banded_attention(qa_rot, apply_rope(kv[:, :, 2, 0], cos, sin), kv[:, :, 2, 1], NSA_WINDOW, None)
        ga = jax.nn.sigmoid(a_gate).reshape(bsz, seq, 3, NSA_KV, NSA_R, 1)
        o_a = (ga[:, :, 0] * o_cmp + ga[:, :, 1] * o_slc + ga[:, :, 2] * o_win).reshape(bsz, seq, BRANCH_W)

        o_b = conformer_conv(b_in, conv_w[l], conv_b[l], conv_ln_g[l], conv_ln_b[l])

        o_c = pool_mixer(c_in, pool_w[l], pool_scale[l])

        qd = apply_rope(d_q.reshape(bsz, seq, SWA_KV, SWA_R, HEAD_DIM), cos, sin)
        kvd = d_kv.reshape(bsz, seq, 2, SWA_KV, HEAD_DIM)
        o_d = banded_attention(qd, apply_rope(kvd[:, :, 0], cos, sin), kvd[:, :, 1], SWA_WINDOW,
                               swa_sinks[l]).reshape(bsz, seq, BRANCH_W)

        branches = jnp.stack([o_a, o_b, o_c, o_d], axis=2)
        up = jnp.einsum("bsnc,ncd->bsnd", branches, w_branch[l])
        gates = jax.nn.sigmoid((h @ w_gate[l]).reshape(bsz, seq, N_BRANCH, D_MODEL))
        mix = jnp.einsum("bsnd,bsnd->bsd", gates, up) @ w_o[l]
        x = x + rms_norm(mix, norm_mix_post[l])

        hf = rms_norm(x, norm_ffn_pre[l])
        u = causal_depthwise_conv(hf @ ffn_w_up[l], ffn_conv_w[l], ffn_conv_b[l])
        gate, val = jnp.split(u, 2, axis=-1)
        f = (jax.nn.gelu(gate, approximate=True) * val) @ ffn_w_down[l]
        x = x + rms_norm(f, norm_ffn_post[l])
    return x
```

```python
import functools

import numpy as np
import jax
import jax.numpy as jnp
from jax import lax
from jax.experimental import pallas as pl
from jax.experimental.pallas import tpu as pltpu

F32 = jnp.float32
BF16 = jnp.bfloat16

D_MODEL = 1024
HEAD_DIM = 64
HALF = HEAD_DIM // 2
ROPE_THETA = 10000.0
NORM_EPS = 1e-6
NEG = -1e30
BRANCH_W = D_MODEL // 2
N_BRANCH = 4
N_HEADS = BRANCH_W // HEAD_DIM
N_KV = 2
N_REP = N_HEADS // N_KV
CMP_LEN = 32
CMP_STRIDE = 16
CMP_HIDDEN = 256
SEL_LEN = 64
SEL_SHIFT = 6
SEL_TOPK = 16
NSA_WINDOW = 512
FORCE_BONUS = 1e3
SWA_WINDOW = 128
CONV_WIDTH = 31
POOL_WINDOWS = (2, 4, 8, 16)
POOL_GROUP_CH = BRANCH_W // len(POOL_WINDOWS)
D_FF = ((8 * D_MODEL // 3) + 127) // 128 * 128
FFN_CONV_WIDTH = 3
IN_SIZES = (BRANCH_W, 3 * 2 * N_KV * HEAD_DIM, 3 * N_HEADS, BRANCH_W, 2 * N_KV * HEAD_DIM, 2 * BRANCH_W, BRANCH_W)

LANES = 128
TM = 256
TQ = 128
TK = 128
HALO = 32
FFN_HALO = 8
FFN_CHUNK = 256
VMEM_LIMIT = 56 * 1024 * 1024

ROPE_COLS = 2 * BRANCH_W + 3 * LANES
COL_SLC_V = ROPE_COLS
COL_WIN_V = COL_SLC_V + LANES
COL_D_V = COL_WIN_V + LANES
COL_CMP_K = COL_D_V + LANES
COL_CMP_V = COL_CMP_K + LANES
COL_B_IN = COL_CMP_V + LANES
COL_C_IN = COL_B_IN + 2 * BRANCH_W
COL_GATE = COL_C_IN + BRANCH_W
N_COLS = COL_GATE + N_KV * LANES


def _in_col_permutation():
    off = np.cumsum((0,) + IN_SIZES)
    a_q, a_kv, a_gate, d_q, d_kv, b_in, c_in = (np.arange(off[i], off[i + 1]) for i in range(7))
    seg = lambda br, kv: a_kv[(br * 2 + kv) * LANES:(br * 2 + kv + 1) * LANES]
    gate = np.full((N_KV, LANES), -1, np.int64)
    for g in range(N_KV):
        for br in range(3):
            for r in range(N_REP):
                gate[g, br * N_REP + r] = a_gate[br * N_HEADS + g * N_REP + r]
    cols = np.concatenate([a_q, d_q, seg(1, 0), seg(2, 0), d_kv[:LANES],
                           seg(1, 1), seg(2, 1), d_kv[LANES:], seg(0, 0), seg(0, 1), b_in, c_in, gate.reshape(-1)])
    assert cols.shape[0] == N_COLS
    return cols


def _const_spec(shape):
    return pl.BlockSpec(shape, lambda *_: (0,) * len(shape), pipeline_mode=pl.Buffered(1))


def _params(n_grid):
    return pltpu.CompilerParams(dimension_semantics=("parallel",) * n_grid, vmem_limit_bytes=VMEM_LIMIT)


def _rms(x, g):
    return x * lax.rsqrt(jnp.mean(x * x, axis=-1, keepdims=True) + NORM_EPS) * g


def _dot(a, b):
    return jnp.dot(a, b, preferred_element_type=F32)


def _dot_t(a, b):
    return lax.dot_general(a, b, (((1,), (1,)), ((), ())), preferred_element_type=F32)


def _proj_kernel(x_ref, g_ref, w_ref, cos_ref, sin_ref, q_ref, kva_ref, kvd_ref, cmp_ref, glu_ref, cin_ref, gate_ref):
    hb = _rms(x_ref[0], g_ref[...]).astype(BF16)
    cos, sin = cos_ref[...], sin_ref[...]
    first_half = (lax.broadcasted_iota(jnp.int32, (TM, LANES), 1) & (HEAD_DIM - 1)) < HALF

    def mm(col, width=LANES):
        return _dot(hb, w_ref[:, col:col + width])

    def rope(z):
        partner = jnp.where(first_half, pltpu.roll(z, LANES - HALF, 1), pltpu.roll(z, HALF, 1))
        return z * cos + partner * sin

    def put_heads(ref, first, z):
        ref[0, first] = z[:, :HEAD_DIM].astype(ref.dtype)
        ref[0, first + 1] = z[:, HEAD_DIM:].astype(ref.dtype)

    scale = HEAD_DIM ** -0.5
    for c in range(BRANCH_W // LANES):
        za = mm(c * LANES) * scale
        put_heads(q_ref, 2 * c, za)
        put_heads(q_ref, N_HEADS + 2 * c, rope(za))
        zd = mm(BRANCH_W + c * LANES) * scale
        put_heads(q_ref, 2 * N_HEADS + 2 * c, rope(zd))
    put_heads(kva_ref, 0, rope(mm(2 * BRANCH_W)))
    put_heads(kva_ref, 2, mm(COL_SLC_V))
    put_heads(kva_ref, 4, rope(mm(2 * BRANCH_W + LANES)))
    put_heads(kva_ref, 6, mm(COL_WIN_V))
    put_heads(kvd_ref, 0, rope(mm(2 * BRANCH_W + 2 * LANES)))
    put_heads(kvd_ref, 2, mm(COL_D_V))
    put_heads(cmp_ref, 0, mm(COL_CMP_K))
    put_heads(cmp_ref, 2, mm(COL_CMP_V))
    glu_ref[0] = mm(COL_B_IN, BRANCH_W) * jax.nn.sigmoid(mm(COL_B_IN + BRANCH_W, BRANCH_W))
    cin_ref[0] = mm(COL_C_IN, BRANCH_W)
    gate_ref[0] = jax.nn.sigmoid(mm(COL_GATE, N_KV * LANES))


def _proj(x, g, w, cos, sin):
    b, s, d = x.shape
    heads = lambda n, dt: (jax.ShapeDtypeStruct((b, n, s, HEAD_DIM), dt),
                           pl.BlockSpec((1, n, TM, HEAD_DIM), lambda i, j: (i, 0, j, 0)))
    rows = lambda n: (jax.ShapeDtypeStruct((b, s, n), F32), pl.BlockSpec((1, TM, n), lambda i, j: (i, j, 0)))
    outs = [heads(3 * N_HEADS, BF16), heads(4 * N_KV, BF16), heads(2 * N_KV, BF16), heads(2 * N_KV, F32),
            rows(BRANCH_W), rows(BRANCH_W), rows(N_KV * LANES)]
    return pl.pallas_call(
        _proj_kernel,
        grid=(b, s // TM),
        in_specs=[pl.BlockSpec((1, TM, d), lambda i, j: (i, j, 0)),
                  _const_spec((1, d)), _const_spec((d, N_COLS)),
                  pl.BlockSpec((TM, LANES), lambda i, j: (j, 0)),
                  pl.BlockSpec((TM, LANES), lambda i, j: (j, 0))],
        out_specs=[o[1] for o in outs],
        out_shape=[o[0] for o in outs],
        compiler_params=_params(2),
    )(x, g, w, cos, sin)


def _compress_kernel(c_ref, pos_ref, w1_ref, w2_ref, o_ref):
    c = c_ref[0, 0]
    n_chunk, half = c.shape
    pos = pos_ref[0]
    top = _dot((c + pos[:, :half]).astype(BF16), w1_ref[0, :half])
    bot = _dot((c + pos[:, half:]).astype(BF16), w1_ref[0, half:])
    hid = top + pltpu.roll(bot, n_chunk - 1, 0)
    out = _dot(jax.nn.gelu(hid, approximate=True).astype(BF16), w2_ref[0])
    row = lax.broadcasted_iota(jnp.int32, out.shape, 0)
    o_ref[0, 0] = jnp.where(row < n_chunk - 1, out, 0.0)


def _compress(chunks, pos, w1, w2):
    b, n, n_chunk, width = chunks.shape
    return pl.pallas_call(
        _compress_kernel,
        grid=(b, n),
        in_specs=[pl.BlockSpec((1, 1, n_chunk, width), lambda i, j: (i, j, 0, 0)),
                  pl.BlockSpec((1, 1, 2 * width), lambda i, j: (j // N_KV, 0, 0)),
                  pl.BlockSpec((1, 2 * width, CMP_HIDDEN), lambda i, j: (j // N_KV, 0, 0)),
                  pl.BlockSpec((1, CMP_HIDDEN, HEAD_DIM), lambda i, j: (j // N_KV, 0, 0))],
        out_specs=pl.BlockSpec((1, 1, n_chunk, HEAD_DIM), lambda i, j: (i, j, 0, 0)),
        out_shape=jax.ShapeDtypeStruct((b, n, n_chunk, HEAD_DIM), F32),
        compiler_params=_params(2),
    )(chunks, pos, w1, w2)


def _online_step(q, k, v, mask, state):
    m, l, acc = state
    s = _dot_t(q, k).reshape(N_REP, TQ, -1)
    s = jnp.where(mask[None], s, NEG)
    m_new = jnp.maximum(m, jnp.max(s, axis=-1, keepdims=True))
    alpha = jnp.exp(m - m_new)
    p = jnp.where(mask[None], jnp.exp(s - m_new), 0.0)
    l = alpha * l + jnp.sum(p, axis=-1, keepdims=True)
    pv = _dot(p.reshape(N_REP * TQ, -1).astype(BF16), v).reshape(N_REP, TQ, HEAD_DIM)
    return m_new, l, alpha * acc + pv


def _online_init():
    return (jnp.full((N_REP, TQ, 1), NEG, F32), jnp.zeros((N_REP, TQ, 1), F32),
            jnp.zeros((N_REP, TQ, HEAD_DIM), F32))


def _band_attention(q, k_ref, v_ref, qi, window, state):
    t = qi * TQ + lax.broadcasted_iota(jnp.int32, (TQ, TK), 0)
    col = lax.broadcasted_iota(jnp.int32, (TQ, TK), 1)
    for d in range(window // TK + 1):
        kt = jnp.maximum(qi - d, 0)
        j = (qi - d) * TK + col
        mask = (j <= t) & (j > t - window) & (j >= 0)
        rows = pl.ds(pl.multiple_of(kt * TK, TK), TK)
        state = _online_step(q, k_ref[0, 0, rows, :], v_ref[0, 0, rows, :], mask, state)
    return state


def _nsa_kernel(q_ref, qr_ref, kc_ref, vc_ref, ks_ref, vs_ref, kw_ref, vw_ref, gate_ref, o_ref):
    qi = pl.program_id(2)
    q = q_ref[0].reshape(N_REP * TQ, HEAD_DIM)
    qr = qr_ref[0].reshape(N_REP * TQ, HEAD_DIM)
    n_chunk = kc_ref.shape[2]
    n_slc = ks_ref.shape[2] // SEL_LEN
    t1 = qi * TQ + lax.broadcasted_iota(jnp.int32, (TQ, 1), 0)

    blk = lax.broadcasted_iota(jnp.int32, (TQ, n_chunk), 1)
    vis = ((blk * CMP_STRIDE + CMP_LEN - 1 <= t1) & (blk < n_chunk - 1))[None]
    s = jnp.where(vis, _dot_t(q, kc_ref[0, 0].astype(BF16)).reshape(N_REP, TQ, n_chunk), NEG)
    e = jnp.exp(s - jnp.max(s, axis=-1, keepdims=True))
    p = jnp.where(vis, e / jnp.sum(e, axis=-1, keepdims=True), 0.0)
    o_cmp = _dot(p.reshape(N_REP * TQ, n_chunk).astype(BF16), vc_ref[0, 0].astype(BF16)).reshape(N_REP, TQ, HEAD_DIM)

    ci = lax.broadcasted_iota(jnp.int32, (n_chunk, LANES), 0)
    sj = lax.broadcasted_iota(jnp.int32, (n_chunk, LANES), 1)
    overlap = ((ci * CMP_STRIDE < (sj + 1) * SEL_LEN) & (ci * CMP_STRIDE + CMP_LEN > sj * SEL_LEN)).astype(F32)
    imp = jnp.dot(jnp.sum(p, axis=0), overlap, preferred_element_type=F32, precision=lax.Precision.HIGHEST)
    sb = lax.broadcasted_iota(jnp.int32, (TQ, LANES), 1)
    cur = t1 >> SEL_SHIFT
    forced = (sb == 0) | (sb == cur) | (sb == cur - 1)
    score = jnp.where(sb <= cur, imp + jnp.where(forced, FORCE_BONUS, 0.0), -1.0)
    score = jnp.where(sb < n_slc, score, -2.0)
    rank = jnp.zeros((TQ, LANES), F32)
    for i in range(n_slc):
        si = score[:, i:i + 1]
        rank += jnp.where((si > score) | ((si == score) & (sb > i)), 1.0, 0.0)
    sel = jnp.where((rank < min(SEL_TOPK, n_slc)) & (sb < n_slc), 1.0, 0.0).astype(BF16)

    t = qi * TQ + lax.broadcasted_iota(jnp.int32, (TQ, TK), 0)
    col = lax.broadcasted_iota(jnp.int32, (TQ, TK), 1)
    eb = lax.broadcasted_iota(jnp.int32, (LANES, TK), 0)
    ec = lax.broadcasted_iota(jnp.int32, (LANES, TK), 1)

    def slc_step(kt, state):
        expand = jnp.where(eb == ((kt * TK + ec) >> SEL_SHIFT), 1.0, 0.0).astype(BF16)
        mask = (_dot(sel, expand) > 0.5) & (kt * TK + col <= t)
        rows = pl.ds(pl.multiple_of(kt * TK, TK), TK)
        return _online_step(qr, ks_ref[0, 0, rows, :], vs_ref[0, 0, rows, :], mask, state)

    _, l_slc, o_slc = lax.fori_loop(0, qi + 1, slc_step, _online_init())
    _, l_win, o_win = _band_attention(qr, kw_ref, vw_ref, qi, NSA_WINDOW, _online_init())

    gate = gate_ref[0]
    o = []
    for r in range(N_REP):
        g = lambda br: gate[:, br * N_REP + r:br * N_REP + r + 1]
        o.append(g(0) * o_cmp[r] + g(1) * (o_slc[r] / l_slc[r]) + g(2) * (o_win[r] / l_win[r]))
    o_ref[0] = jnp.concatenate(o, axis=-1)


def _nsa(q_all, kc, kv, gates):
    b, _, s, _ = q_all.shape
    n_chunk = kc.shape[2]
    q_spec = lambda first: pl.BlockSpec((1, N_REP, TQ, HEAD_DIM), lambda i, g, j: (i, first + g, j, 0))
    c_spec = lambda first: pl.BlockSpec((1, 1, n_chunk, HEAD_DIM), lambda i, g, j: (i, first + g, 0, 0))
    kv_spec = lambda first: pl.BlockSpec((1, 1, s, HEAD_DIM), lambda i, g, j: (i, first + g, 0, 0))
    return pl.pallas_call(
        _nsa_kernel,
        grid=(b, N_KV, s // TQ),
        in_specs=[q_spec(0), q_spec(N_KV), c_spec(0), c_spec(N_KV),
                  kv_spec(0), kv_spec(N_KV), kv_spec(2 * N_KV), kv_spec(3 * N_KV),
                  pl.BlockSpec((1, TQ, LANES), lambda i, g, j: (i, j, g))],
        out_specs=pl.BlockSpec((1, TQ, N_REP * HEAD_DIM), lambda i, g, j: (i, j, g)),
        out_shape=jax.ShapeDtypeStruct((b, s, BRANCH_W), F32),
        compiler_params=_params(3),
    )(q_all, q_all, kc, kc, kv, kv, kv, kv, gates)


def _swa_kernel(q_ref, k_ref, v_ref, sink_ref, o_ref):
    qi = pl.program_id(2)
    q = q_ref[0].reshape(N_REP * TQ, HEAD_DIM)
    m, l, acc = _band_attention(q, k_ref, v_ref, qi, SWA_WINDOW, _online_init())
    sink = jnp.stack([jnp.broadcast_to(sink_ref[0, r:r + 1, :1], (TQ, 1)) for r in range(N_REP)])
    m_new = jnp.maximum(m, sink)
    alpha = jnp.exp(m - m_new)
    out = (alpha * acc) / (alpha * l + jnp.exp(sink - m_new))
    o_ref[0] = jnp.concatenate([out[r] for r in range(N_REP)], axis=-1)


def _swa(q_all, kv, sinks):
    b, _, s, _ = q_all.shape
    kv_spec = lambda first: pl.BlockSpec((1, 1, s, HEAD_DIM), lambda i, g, j: (i, first + g, 0, 0))
    return pl.pallas_call(
        _swa_kernel,
        grid=(b, N_KV, s // TQ),
        in_specs=[pl.BlockSpec((1, N_REP, TQ, HEAD_DIM), lambda i, g, j: (i, 2 * N_KV + g, j, 0)),
                  kv_spec(0), kv_spec(N_KV),
                  pl.BlockSpec((1, N_REP, LANES), lambda i, g, j: (g, 0, 0))],
        out_specs=pl.BlockSpec((1, TQ, N_REP * HEAD_DIM), lambda i, g, j: (i, j, g)),
        out_shape=jax.ShapeDtypeStruct((b, s, BRANCH_W), F32),
        compiler_params=_params(3),
    )(q_all, kv, kv, sinks)


def _convpool_kernel(glu_ref, glu_prev_ref, cin_ref, cin_prev_ref, cw_ref, cb_ref, lg_ref, lb_ref, pw_ref, ps_ref,
                     ob_ref, oc_ref, gext, cext):
    first = pl.program_id(1) == 0
    gext[:HALO] = jnp.where(first, 0.0, glu_prev_ref[0])
    gext[HALO:] = glu_ref[0]
    cext[:HALO] = jnp.where(first, 0.0, cin_prev_ref[0])
    cext[HALO:] = cin_ref[0]

    acc = jnp.zeros((TM, BRANCH_W), F32) + cb_ref[...]
    for k in range(CONV_WIDTH):
        acc += cw_ref[k:k + 1, :] * gext[pl.ds(HALO - (CONV_WIDTH - 1) + k, TM), :]
    mu = jnp.mean(acc, axis=-1, keepdims=True)
    cen = acc - mu
    y = cen * lax.rsqrt(jnp.mean(cen * cen, axis=-1, keepdims=True) + NORM_EPS) * lg_ref[...] + lb_ref[...]
    ob_ref[0] = y * jax.nn.sigmoid(y)

    t = pl.program_id(1) * TM + lax.broadcasted_iota(jnp.int32, (TM, 1), 0)
    for g, win in enumerate(POOL_WINDOWS):
        lanes = pl.ds(g * POOL_GROUP_CH, POOL_GROUP_CH)
        tot = cext[pl.ds(HALO, TM), lanes]
        for d in range(1, win):
            tot += cext[pl.ds(HALO - d, TM), lanes]
        pooled = tot / jnp.minimum(t + 1, win).astype(F32) - cext[pl.ds(HALO, TM), lanes]
        oc_ref[0, :, lanes] = _dot(pooled.astype(BF16), pw_ref[g]) * ps_ref[:, lanes]


def _convpool(glu, cin, cw, cb, lg, lb, pw, ps):
    b, s, w = glu.shape
    cur = pl.BlockSpec((1, TM, w), lambda i, j: (i, j, 0))
    prev = pl.BlockSpec((1, HALO, w), lambda i, j: (i, jnp.maximum(j * (TM // HALO) - 1, 0), 0))
    return pl.pallas_call(
        _convpool_kernel,
        grid=(b, s // TM),
        in_specs=[cur, prev, cur, prev, _const_spec(cw.shape), _const_spec(cb.shape), _const_spec(lg.shape),
                  _const_spec(lb.shape), _const_spec(pw.shape), _const_spec(ps.shape)],
        out_specs=[cur, cur],
        out_shape=[jax.ShapeDtypeStruct((b, s, w), F32)] * 2,
        scratch_shapes=[pltpu.VMEM((HALO + TM, w), F32)] * 2,
        compiler_params=_params(2),
    )(glu, glu, cin, cin, cw, cb, lg, lb, pw, ps)


def _merge_kernel(x_ref, oa_ref, ob_ref, oc_ref, od_ref, gpre_ref, gpost_ref, wg_ref, wb_ref, wo_ref, o_ref):
    x = x_ref[0]
    hb = _rms(x, gpre_ref[...]).astype(BF16)
    mix = jnp.zeros((TM, D_MODEL), F32)
    for n, br_ref in enumerate((oa_ref, ob_ref, oc_ref, od_ref)):
        gate = jax.nn.sigmoid(_dot(hb, wg_ref[:, n * D_MODEL:(n + 1) * D_MODEL]))
        mix += gate * _dot(br_ref[0].astype(BF16), wb_ref[n])
    o_ref[0] = x + _rms(_dot(mix.astype(BF16), wo_ref[...]), gpost_ref[...])


def _merge(x, oa, ob, oc, od, gpre, gpost, wg, wb, wo):
    b, s, d = x.shape
    xs = pl.BlockSpec((1, TM, d), lambda i, j: (i, j, 0))
    br = pl.BlockSpec((1, TM, BRANCH_W), lambda i, j: (i, j, 0))
    return pl.pallas_call(
        _merge_kernel,
        grid=(b, s // TM),
        in_specs=[xs, br, br, br, br, _const_spec(gpre.shape), _const_spec(gpost.shape),
                  _const_spec(wg.shape), _const_spec(wb.shape), _const_spec(wo.shape)],
        out_specs=xs,
        out_shape=jax.ShapeDtypeStruct(x.shape, F32),
        compiler_params=_params(2),
    )(x, oa, ob, oc, od, gpre, gpost, wg, wb, wo)


def _ffn_kernel(x_ref, xprev_ref, gpre_ref, gpost_ref, wu_ref, cw_ref, cb_ref, wd_ref, o_ref):
    x = x_ref[0]
    first = pl.program_id(1) == 0
    xe = jnp.concatenate([jnp.where(first, 0.0, xprev_ref[0]), x], axis=0)
    hb = _rms(xe, gpre_ref[...]).astype(BF16)

    def conv(col):
        u = _dot(hb, wu_ref[:, col:col + FFN_CHUNK])
        out = cb_ref[:, col:col + FFN_CHUNK]
        for k in range(FFN_CONV_WIDTH):
            lo = FFN_HALO - (FFN_CONV_WIDTH - 1) + k
            out = out + cw_ref[k:k + 1, col:col + FFN_CHUNK] * u[lo:lo + TM]
        return out

    f = jnp.zeros((TM, D_MODEL), F32)
    for c in range(D_FF // FFN_CHUNK):
        act = jax.nn.gelu(conv(c * FFN_CHUNK), approximate=True) * conv(D_FF + c * FFN_CHUNK)
        f += _dot(act.astype(BF16), wd_ref[c * FFN_CHUNK:(c + 1) * FFN_CHUNK, :])
    o_ref[0] = x + _rms(f, gpost_ref[...])


def _ffn(x, gpre, gpost, wu, cw, cb, wd):
    b, s, d = x.shape
    xs = pl.BlockSpec((1, TM, d), lambda i, j: (i, j, 0))
    prev = pl.BlockSpec((1, FFN_HALO, d), lambda i, j: (i, jnp.maximum(j * (TM // FFN_HALO) - 1, 0), 0))
    return pl.pallas_call(
        _ffn_kernel,
        grid=(b, s // TM),
        in_specs=[xs, prev, _const_spec(gpre.shape), _const_spec(gpost.shape), _const_spec(wu.shape),
                  _const_spec(cw.shape), _const_spec(cb.shape), _const_spec(wd.shape)],
        out_specs=xs,
        out_shape=jax.ShapeDtypeStruct(x.shape, F32),
        compiler_params=_params(2),
    )(x, x, gpre, gpost, wu, cw, cb, wd)


def _rope_tables(seq):
    inv = 1.0 / (ROPE_THETA ** (jnp.arange(0, HEAD_DIM, 2, dtype=F32) / HEAD_DIM))
    ang = jnp.arange(seq, dtype=F32)[:, None] * inv[None, :]
    cos, sin = jnp.cos(ang), jnp.sin(ang)
    reps = LANES // HEAD_DIM
    return jnp.tile(jnp.concatenate([cos, cos], -1), (1, reps)), jnp.tile(jnp.concatenate([-sin, sin], -1), (1, reps))


def kernel(x, norm_mix_pre, norm_mix_post, norm_ffn_pre, norm_ffn_post, w_in, nsa_cmp_pos, nsa_cmp_w1, nsa_cmp_w2, swa_sinks, conv_w, conv_b, conv_ln_g, conv_ln_b, pool_w, pool_scale, w_branch, w_gate, w_o, ffn_w_up, ffn_conv_w, ffn_conv_b, ffn_w_down):
    bsz, seq, d = x.shape
    assert d == D_MODEL and seq % TM == 0 and seq // SEL_LEN <= LANES and seq // CMP_STRIDE <= LANES
    depth = w_in.shape[0]
    cos, sin = _rope_tables(seq)
    perm = _in_col_permutation()
    row = lambda v: v.reshape(1, -1)
    for l in range(depth):
        w_pad = jnp.concatenate([w_in[l], jnp.zeros((d, 1), w_in.dtype)], axis=1)
        w_perm = jnp.take(w_pad, jnp.asarray(np.where(perm < 0, w_in.shape[2], perm)), axis=1).astype(BF16)
        q_all, kv_a, kv_d, cmp_raw, glu, cin, gates = _proj(x, row(norm_mix_pre[l]), w_perm, cos, sin)

        chunks = cmp_raw.reshape(bsz, 2 * N_KV, seq // CMP_STRIDE, CMP_STRIDE * HEAD_DIM)
        kc = _compress(chunks, nsa_cmp_pos[l].reshape(2, 1, CMP_LEN * HEAD_DIM),
                       nsa_cmp_w1[l].astype(BF16), nsa_cmp_w2[l].astype(BF16))
        o_a = _nsa(q_all, kc, kv_a, gates)
        sinks = jnp.broadcast_to(swa_sinks[l].reshape(N_KV, N_REP, 1), (N_KV, N_REP, LANES))
        o_d = _swa(q_all, kv_d, sinks)
        o_b, o_c = _convpool(glu, cin, conv_w[l], row(conv_b[l]), row(conv_ln_g[l]), row(conv_ln_b[l]),
                             pool_w[l].astype(BF16), row(pool_scale[l]))
        x = _merge(x, o_a, o_b, o_c, o_d, row(norm_mix_pre[l]), row(norm_mix_post[l]),
                   w_gate[l].astype(BF16), w_branch[l].astype(BF16), w_o[l].astype(BF16))
        x = _ffn(x, row(norm_ffn_pre[l]), row(norm_ffn_post[l]), ffn_w_up[l].astype(BF16),
                 ffn_conv_w[l], row(ffn_conv_b[l]), ffn_w_down[l].astype(BF16))
    return x
```

```python
import numpy as np
import jax
import jax.numpy as jnp
from jax import lax
from jax.experimental import pallas as pl
from jax.experimental.pallas import tpu as pltpu

F32 = jnp.float32
BF16 = jnp.bfloat16

D_MODEL = 1024
HEAD_DIM = 64
HALF = HEAD_DIM // 2
ROPE_THETA = 10000.0
NORM_EPS = 1e-6
NEG = -1e30
M_INIT = -1e29
BRANCH_W = D_MODEL // 2
N_BRANCH = 4
N_HEADS = BRANCH_W // HEAD_DIM
N_KV = 2
N_REP = N_HEADS // N_KV
CMP_LEN = 32
CMP_STRIDE = 16
CMP_HIDDEN = 256
SEL_LEN = 64
SEL_SHIFT = 6
SEL_TOPK = 16
NSA_WINDOW = 512
FORCE_BONUS = 1e3
SWA_WINDOW = 128
CONV_WIDTH = 31
POOL_WINDOWS = (2, 4, 8, 16)
POOL_GROUP_CH = BRANCH_W // len(POOL_WINDOWS)
D_FF = ((8 * D_MODEL // 3) + 127) // 128 * 128
FFN_CONV_WIDTH = 3
IN_SIZES = (BRANCH_W, 3 * 2 * N_KV * HEAD_DIM, 3 * N_HEADS, BRANCH_W, 2 * N_KV * HEAD_DIM, 2 * BRANCH_W, BRANCH_W)

LANES = 128
SUBLANES = 8
TM = 256
TQ = 256
TK = 256
QL = N_REP * TQ
SWA_TQ = 128
ONES_ROWS = 16
LOG2E = 1.4426950408889634
HALO = 32
FFN_HALO = 8
FFN_CHUNK = 256
VMEM_LIMIT = 56 * 1024 * 1024

ROPE_COLS = 2 * BRANCH_W + 3 * LANES
COL_CMP_K = ROPE_COLS
COL_CMP_V = COL_CMP_K + LANES
COL_GATE = COL_CMP_V + LANES
COL_B_IN = COL_GATE + N_KV * LANES
COL_C_IN = COL_B_IN + 2 * BRANCH_W
N_COLS = COL_C_IN + BRANCH_W
MXU_COLS = 256
N_VT = 3 * N_KV


def _in_col_permutation():
    off = np.cumsum((0,) + IN_SIZES)
    a_q, a_kv, a_gate, d_q, d_kv, b_in, c_in = (np.arange(off[i], off[i + 1]) for i in range(7))
    seg = lambda br, kv: a_kv[(br * 2 + kv) * LANES:(br * 2 + kv + 1) * LANES]
    gate = np.full((N_KV, LANES), -1, np.int64)
    for g in range(N_KV):
        for br in range(3):
            for r in range(N_REP):
                gate[g, br * N_REP + r] = a_gate[br * N_HEADS + g * N_REP + r]
    cols = np.concatenate([a_q, d_q, seg(1, 0), seg(2, 0), d_kv[:LANES],
                           seg(0, 0), seg(0, 1), gate.reshape(-1), b_in, c_in])
    assert cols.shape[0] == N_COLS
    return cols, np.concatenate([seg(1, 1), seg(2, 1), d_kv[LANES:]])


def _const_spec(shape):
    return pl.BlockSpec(shape, lambda *_: (0,) * len(shape), pipeline_mode=pl.Buffered(1))


def _params(n_grid):
    return pltpu.CompilerParams(dimension_semantics=("parallel",) * n_grid, vmem_limit_bytes=VMEM_LIMIT)


def _rms(x, g):
    return x * lax.rsqrt(jnp.mean(x * x, axis=-1, keepdims=True) + NORM_EPS) * g


def _dot(a, b):
    return jnp.dot(a, b, preferred_element_type=F32)


def _dot_t(a, b):
    return lax.dot_general(a, b, (((1,), (1,)), ((), ())), preferred_element_type=F32)


def _proj_kernel(x_ref, g_ref, w_ref, wvt_ref, cos_ref, sin_ref,
                 q_ref, ka_ref, kd_ref, cmp_ref, vta_ref, vtd_ref, glu_ref, cin_ref, gate_ref):
    hb = _rms(x_ref[0], g_ref[...]).astype(BF16)
    cos, sin = cos_ref[...], sin_ref[...]
    first_half = (lax.broadcasted_iota(jnp.int32, (TM, LANES), 1) & (HEAD_DIM - 1)) < HALF

    def mm(col, width=LANES):
        return _dot(hb, w_ref[:, col:col + width])

    def rope(z):
        partner = jnp.where(first_half, pltpu.roll(z, LANES - HALF, 1), pltpu.roll(z, HALF, 1))
        return z * cos + partner * sin

    def put_heads(ref, first, z):
        ref[0, first] = z[:, :HEAD_DIM].astype(ref.dtype)
        ref[0, first + 1] = z[:, HEAD_DIM:].astype(ref.dtype)

    scale = HEAD_DIM ** -0.5 * LOG2E

    def nsa_q(c):
        def put(z):
            put_heads(q_ref, 2 * c, z * scale)
            put_heads(q_ref, N_HEADS + 2 * c, rope(z * scale))
        return put

    def gate(g):
        def put(z):
            gate_ref[0, :, g * LANES:(g + 1) * LANES] = jax.nn.sigmoid(z)
        return put

    segments = [nsa_q(c) for c in range(BRANCH_W // LANES)]
    segments += [lambda z, c=c: put_heads(q_ref, 2 * N_HEADS + 2 * c, rope(z * scale))
                 for c in range(BRANCH_W // LANES)]
    segments += [lambda z: put_heads(ka_ref, 0, rope(z)),
                 lambda z: put_heads(ka_ref, 2, rope(z)),
                 lambda z: put_heads(kd_ref, 0, rope(z)),
                 lambda z: put_heads(cmp_ref, 0, z),
                 lambda z: put_heads(cmp_ref, 2, z)]
    segments += [gate(g) for g in range(N_KV)]
    per_dot = MXU_COLS // LANES
    for first in range(0, len(segments), per_dot):
        group = segments[first:first + per_dot]
        z = mm(first * LANES, len(group) * LANES)
        for i, put in enumerate(group):
            put(z[:, i * LANES:(i + 1) * LANES])
    vt = _dot_t(wvt_ref[...], hb).astype(BF16)
    for n in range(2 * N_KV):
        for c in range(TM // TK):
            vta_ref[0, n, c] = vt[n * HEAD_DIM:(n + 1) * HEAD_DIM, c * TK:(c + 1) * TK]
    for n in range(N_KV):
        rows = slice((2 * N_KV + n) * HEAD_DIM, (2 * N_KV + n + 1) * HEAD_DIM)
        for c in range(TM // SWA_TQ):
            vtd_ref[0, n, c] = vt[rows, c * SWA_TQ:(c + 1) * SWA_TQ]
    glu_ref[0] = mm(COL_B_IN, BRANCH_W) * jax.nn.sigmoid(mm(COL_B_IN + BRANCH_W, BRANCH_W))
    cin_ref[0] = mm(COL_C_IN, BRANCH_W)


def _proj(x, g, w, wvt, cos, sin):
    b, s, d = x.shape
    heads = lambda n, dt: (jax.ShapeDtypeStruct((b, n, s, HEAD_DIM), dt),
                           pl.BlockSpec((1, n, TM, HEAD_DIM), lambda i, j: (i, 0, j, 0)))
    rows = lambda n: (jax.ShapeDtypeStruct((b, s, n), F32), pl.BlockSpec((1, TM, n), lambda i, j: (i, j, 0)))
    vts = lambda n, t: (jax.ShapeDtypeStruct((b, n, s // t, HEAD_DIM, t), BF16),
                        pl.BlockSpec((1, n, TM // t, HEAD_DIM, t), lambda i, j: (i, 0, j, 0, 0)))
    outs = [heads(3 * N_HEADS, BF16), heads(2 * N_KV, BF16), heads(N_KV, BF16), heads(2 * N_KV, F32),
            vts(2 * N_KV, TK), vts(N_KV, SWA_TQ), rows(BRANCH_W), rows(BRANCH_W), rows(N_KV * LANES)]
    return pl.pallas_call(
        _proj_kernel,
        grid=(b, s // TM),
        in_specs=[pl.BlockSpec((1, TM, d), lambda i, j: (i, j, 0)),
                  _const_spec((1, d)), _const_spec(w.shape), _const_spec(wvt.shape),
                  pl.BlockSpec((TM, LANES), lambda i, j: (j, 0)),
                  pl.BlockSpec((TM, LANES), lambda i, j: (j, 0))],
        out_specs=[o[1] for o in outs],
        out_shape=[o[0] for o in outs],
        compiler_params=_params(2),
    )(x, g, w, wvt, cos, sin)


def _compress_kernel(c_ref, pos_ref, w1_ref, w2_ref, w2t_ref, o_ref, ot_ref):
    c = c_ref[0, 0]
    n_chunk, half = c.shape
    pos = pos_ref[0]
    top = _dot((c + pos[:, :half]).astype(BF16), w1_ref[0, :half])
    bot = _dot((c + pos[:, half:]).astype(BF16), w1_ref[0, half:])
    hid = top + pltpu.roll(bot, n_chunk - 1, 0)
    act = jax.nn.gelu(hid, approximate=True).astype(BF16)
    out = _dot(act, w2_ref[0])
    row = lax.broadcasted_iota(jnp.int32, out.shape, 0)
    o_ref[0, 0] = jnp.where(row < n_chunk - 1, out, 0.0)
    out_t = _dot_t(w2t_ref[0], act)
    col = lax.broadcasted_iota(jnp.int32, out_t.shape, 1)
    ot_ref[0, 0] = jnp.where(col < n_chunk - 1, out_t, 0.0)


def _compress(chunks, pos, w1, w2, w2t):
    b, n, n_chunk, width = chunks.shape
    return pl.pallas_call(
        _compress_kernel,
        grid=(b, n),
        in_specs=[pl.BlockSpec((1, 1, n_chunk, width), lambda i, j: (i, j, 0, 0)),
                  pl.BlockSpec((1, 1, 2 * width), lambda i, j: (j // N_KV, 0, 0)),
                  pl.BlockSpec((1, 2 * width, CMP_HIDDEN), lambda i, j: (j // N_KV, 0, 0)),
                  pl.BlockSpec((1, CMP_HIDDEN, HEAD_DIM), lambda i, j: (j // N_KV, 0, 0)),
                  pl.BlockSpec((1, HEAD_DIM, CMP_HIDDEN), lambda i, j: (j // N_KV, 0, 0))],
        out_specs=[pl.BlockSpec((1, 1, n_chunk, HEAD_DIM), lambda i, j: (i, j, 0, 0)),
                   pl.BlockSpec((1, 1, HEAD_DIM, n_chunk), lambda i, j: (i, j, 0, 0))],
        out_shape=[jax.ShapeDtypeStruct((b, n, n_chunk, HEAD_DIM), F32),
                   jax.ShapeDtypeStruct((b, n, HEAD_DIM, n_chunk), F32)],
        compiler_params=_params(2),
    )(chunks, pos, w1, w2, w2t)


def _heads_on_lanes(a):
    return jnp.concatenate([a] * N_REP, axis=1)


def _with_ones(vt):
    return jnp.concatenate([vt, jnp.ones((ONES_ROWS, vt.shape[1]), vt.dtype)], axis=0)


def _online_step(k, vt, q, bias, state):
    m, acc = state
    s = _dot_t(k, q) + bias
    m_new = jnp.maximum(m, jnp.max(s, axis=0, keepdims=True))
    p = jnp.exp2(s - m_new).astype(BF16)
    return m_new, jnp.exp2(m - m_new) * acc + _dot(_with_ones(vt), p)


def _online_init():
    return jnp.full((1, QL), M_INIT, F32), jnp.zeros((HEAD_DIM + ONES_ROWS, QL), F32)


def _band_attention(q, k_ref, vt_ref, qi, window, state):
    tq = qi * TQ + lax.broadcasted_iota(jnp.int32, (TK, TQ), 1)
    row = lax.broadcasted_iota(jnp.int32, (TK, TQ), 0)

    def step(d, state):
        kt = qi - d
        key = kt * TK + row
        bias = _heads_on_lanes(jnp.where((key <= tq) & (key > tq - window), 0.0, NEG))
        rows = pl.ds(pl.multiple_of(kt * TK, TK), TK)
        return _online_step(k_ref[0, 0, rows, :], vt_ref[0, 0, kt], q, bias, state)

    return lax.fori_loop(0, jnp.minimum(qi, window // TK) + 1, step, state)


def _heads_to_rows(o_t):
    tq = o_t.shape[1] // N_REP
    return jnp.concatenate([o_t[:, r * tq:(r + 1) * tq] for r in range(N_REP)], axis=0).T


def _nsa_kernel(q_ref, qr_ref, kc_ref, vct_ref, ks_ref, vst_ref, kw_ref, vwt_ref, gate_ref, o_ref, selb_ref):
    qi = pl.program_id(2)
    q = q_ref[0].reshape(QL, HEAD_DIM)
    qr = qr_ref[0].reshape(QL, HEAD_DIM)
    n_chunk = kc_ref.shape[2]
    n_slc = selb_ref.shape[0]

    blk = lax.broadcasted_iota(jnp.int32, (n_chunk, TQ), 0)
    tqc = qi * TQ + lax.broadcasted_iota(jnp.int32, (n_chunk, TQ), 1)
    vis = (blk * CMP_STRIDE + CMP_LEN - 1 <= tqc) & (blk < n_chunk - 1)
    s = _dot_t(kc_ref[0, 0].astype(BF16), q) + _heads_on_lanes(jnp.where(vis, 0.0, NEG))
    e = jnp.exp2(s - jnp.maximum(jnp.max(s, axis=0, keepdims=True), M_INIT))
    p = e * (1.0 / jnp.maximum(jnp.sum(e, axis=0, keepdims=True), 1e-30))
    o_cmp = _dot(vct_ref[0, 0].astype(BF16), p.astype(BF16))

    p_sum = p[:, :TQ]
    for r in range(1, N_REP):
        p_sum = p_sum + p[:, r * TQ:(r + 1) * TQ]
    oj = lax.broadcasted_iota(jnp.int32, (n_slc, n_chunk), 0)
    oi = lax.broadcasted_iota(jnp.int32, (n_slc, n_chunk), 1)
    overlap = ((oi * CMP_STRIDE < (oj + 1) * SEL_LEN) & (oi * CMP_STRIDE + CMP_LEN > oj * SEL_LEN)).astype(F32)
    imp = jnp.dot(overlap, p_sum, preferred_element_type=F32, precision=lax.Precision.HIGHEST)
    sb = lax.broadcasted_iota(jnp.int32, (n_slc, TQ), 0)
    cur = (qi * TQ + lax.broadcasted_iota(jnp.int32, (n_slc, TQ), 1)) >> SEL_SHIFT
    forced = (sb == 0) | (sb == cur) | (sb == cur - 1)
    score = jnp.where(sb <= cur, imp + jnp.where(forced, FORCE_BONUS, 0.0), -1.0)
    rank = jnp.zeros((n_slc, TQ), F32)
    for i in range(n_slc):
        si = score[i:i + 1, :]
        rank += jnp.where((si > score) | ((si == score) & (sb > i)), 1.0, 0.0)
    selb_ref[...] = jnp.where(rank < min(SEL_TOPK, n_slc), 0.0, NEG)

    tq = qi * TQ + lax.broadcasted_iota(jnp.int32, (TK, TQ), 1)
    row = lax.broadcasted_iota(jnp.int32, (TK, TQ), 0)
    per_tile = TK // SEL_LEN

    def slc_step(kt, state):
        blocks = [jnp.broadcast_to(selb_ref[pl.ds(kt * per_tile + i, 1), :], (SEL_LEN, TQ)) for i in range(per_tile)]
        bias = jnp.where(kt * TK + row <= tq, jnp.concatenate(blocks, axis=0), NEG)
        rows = pl.ds(pl.multiple_of(kt * TK, TK), TK)
        return _online_step(ks_ref[0, 0, rows, :], vst_ref[0, 0, kt], qr, _heads_on_lanes(bias), state)

    _, slc = lax.fori_loop(0, qi + 1, slc_step, _online_init())
    _, win = _band_attention(qr, kw_ref, vwt_ref, qi, NSA_WINDOW, _online_init())

    gate_t = gate_ref[0].T
    g = lambda br: jnp.concatenate([gate_t[br * N_REP + r:br * N_REP + r + 1, :] for r in range(N_REP)], axis=1)
    normed = lambda acc: acc[:HEAD_DIM] * (1.0 / acc[HEAD_DIM:HEAD_DIM + 1])
    o_ref[0] = _heads_to_rows(g(0) * o_cmp + g(1) * normed(slc) + g(2) * normed(win))


def _nsa(q_all, kc, kct, k_a, vt, gates):
    b, _, s, _ = q_all.shape
    n_chunk = kc.shape[2]
    q_spec = lambda first: pl.BlockSpec((1, N_REP, TQ, HEAD_DIM), lambda i, g, j: (i, first + g, j, 0))
    k_spec = lambda first: pl.BlockSpec((1, 1, s, HEAD_DIM), lambda i, g, j: (i, first + g, 0, 0))
    vt_spec = lambda first: pl.BlockSpec((1, 1, s // TK, HEAD_DIM, TK), lambda i, g, j: (i, first + g, 0, 0, 0))
    return pl.pallas_call(
        _nsa_kernel,
        grid=(b, N_KV, s // TQ),
        in_specs=[q_spec(0), q_spec(N_KV),
                  pl.BlockSpec((1, 1, n_chunk, HEAD_DIM), lambda i, g, j: (i, g, 0, 0)),
                  pl.BlockSpec((1, 1, HEAD_DIM, n_chunk), lambda i, g, j: (i, N_KV + g, 0, 0)),
                  k_spec(0), vt_spec(0), k_spec(N_KV), vt_spec(N_KV),
                  pl.BlockSpec((1, TQ, LANES), lambda i, g, j: (i, j, g))],
        out_specs=pl.BlockSpec((1, TQ, N_REP * HEAD_DIM), lambda i, g, j: (i, j, g)),
        out_shape=jax.ShapeDtypeStruct((b, s, BRANCH_W), F32),
        scratch_shapes=[pltpu.VMEM((s // SEL_LEN, TQ), F32)],
        compiler_params=_params(3),
    )(q_all, q_all, kc, kct, k_a, vt, k_a, vt, gates)


def _swa_kernel(q_ref, k_ref, vt_ref, sink_ref, o_ref):
    sink = jnp.concatenate([sink_ref[0, r:r + 1, :] for r in range(N_REP)], axis=1) * LOG2E
    row = lax.broadcasted_iota(jnp.int32, (2 * SWA_TQ, SWA_TQ), 0)
    col = lax.broadcasted_iota(jnp.int32, (2 * SWA_TQ, SWA_TQ), 1)
    for sub in range(TQ // SWA_TQ):
        qt = pl.program_id(2) * (TQ // SWA_TQ) + sub
        kt = jnp.maximum(qt - 1, 0)
        q = q_ref[0, :, sub * SWA_TQ:(sub + 1) * SWA_TQ, :].reshape(N_REP * SWA_TQ, HEAD_DIM)
        key, tq = kt * SWA_TQ + row, qt * SWA_TQ + col
        bias = _heads_on_lanes(jnp.where((key <= tq) & (key > tq - SWA_WINDOW), 0.0, NEG))
        s = _dot_t(k_ref[0, 0, pl.ds(pl.multiple_of(kt * SWA_TQ, SWA_TQ), 2 * SWA_TQ), :], q) + bias
        m = jnp.maximum(jnp.max(s, axis=0, keepdims=True), sink)
        vt = jnp.concatenate([vt_ref[0, 0, kt], vt_ref[0, 0, kt + 1]], axis=1)
        acc = _dot(_with_ones(vt), jnp.exp2(s - m).astype(BF16))
        out = acc[:HEAD_DIM] * (1.0 / (acc[HEAD_DIM:HEAD_DIM + 1] + jnp.exp2(sink - m)))
        o_ref[0, sub * SWA_TQ:(sub + 1) * SWA_TQ, :] = _heads_to_rows(out)


def _swa(q_all, k_d, vt, sinks):
    b, _, s, _ = q_all.shape
    return pl.pallas_call(
        _swa_kernel,
        grid=(b, N_KV, s // TQ),
        in_specs=[pl.BlockSpec((1, N_REP, TQ, HEAD_DIM), lambda i, g, j: (i, 2 * N_KV + g, j, 0)),
                  pl.BlockSpec((1, 1, s, HEAD_DIM), lambda i, g, j: (i, g, 0, 0)),
                  pl.BlockSpec((1, 1, s // SWA_TQ, HEAD_DIM, SWA_TQ), lambda i, g, j: (i, g, 0, 0, 0)),
                  pl.BlockSpec((1, N_REP, SWA_TQ), lambda i, g, j: (g, 0, 0))],
        out_specs=pl.BlockSpec((1, TQ, N_REP * HEAD_DIM), lambda i, g, j: (i, j, g)),
        out_shape=jax.ShapeDtypeStruct((b, s, BRANCH_W), F32),
        compiler_params=_params(3),
    )(q_all, k_d, vt, sinks)


def _convpool_kernel(glu_ref, glu_prev_ref, cin_ref, cin_prev_ref, cw_ref, cb_ref, lg_ref, lb_ref, pw_ref, ps_ref,
                     ob_ref, oc_ref, gext, gshift, cext):
    first = pl.program_id(1) == 0
    gext[:HALO] = jnp.where(first, 0.0, glu_prev_ref[0])
    gext[HALO:] = glu_ref[0]
    cext[:HALO] = jnp.where(first, 0.0, cin_prev_ref[0])
    cext[HALO:] = cin_ref[0]
    n_shift = HALO + TM - SUBLANES
    for ph in range(1, SUBLANES):
        gshift[ph - 1, :n_shift] = gext[pl.ds(ph, n_shift), :]

    acc = jnp.zeros((TM, BRANCH_W), F32) + cb_ref[...]
    for k in range(CONV_WIDTH):
        base, ph = divmod(HALO - (CONV_WIDTH - 1) + k, SUBLANES)
        rows = pl.ds(base * SUBLANES, TM)
        acc += cw_ref[k:k + 1, :] * (gext[rows, :] if ph == 0 else gshift[ph - 1, rows, :])
    mu = jnp.mean(acc, axis=-1, keepdims=True)
    cen = acc - mu
    y = cen * lax.rsqrt(jnp.mean(cen * cen, axis=-1, keepdims=True) + NORM_EPS) * lg_ref[...] + lb_ref[...]
    ob_ref[0] = y * jax.nn.sigmoid(y)

    t = pl.program_id(1) * TM + lax.broadcasted_iota(jnp.int32, (TM, 1), 0)
    for g, win in enumerate(POOL_WINDOWS):
        lanes = pl.ds(g * POOL_GROUP_CH, POOL_GROUP_CH)
        tot = cext[pl.ds(HALO, TM), lanes]
        for d in range(1, win):
            tot += cext[pl.ds(HALO - d, TM), lanes]
        pooled = tot / jnp.minimum(t + 1, win).astype(F32) - cext[pl.ds(HALO, TM), lanes]
        oc_ref[0, :, lanes] = _dot(pooled.astype(BF16), pw_ref[g]) * ps_ref[:, lanes]


def _convpool(glu, cin, cw, cb, lg, lb, pw, ps):
    b, s, w = glu.shape
    cur = pl.BlockSpec((1, TM, w), lambda i, j: (i, j, 0))
    prev = pl.BlockSpec((1, HALO, w), lambda i, j: (i, jnp.maximum(j * (TM // HALO) - 1, 0), 0))
    return pl.pallas_call(
        _convpool_kernel,
        grid=(b, s // TM),
        in_specs=[cur, prev, cur, prev, _const_spec(cw.shape), _const_spec(cb.shape), _const_spec(lg.shape),
                  _const_spec(lb.shape), _const_spec(pw.shape), _const_spec(ps.shape)],
        out_specs=[cur, cur],
        out_shape=[jax.ShapeDtypeStruct((b, s, w), F32)] * 2,
        scratch_shapes=[pltpu.VMEM((HALO + TM, w), F32), pltpu.VMEM((SUBLANES - 1, HALO + TM, w), F32),
                        pltpu.VMEM((HALO + TM, w), F32)],
        compiler_params=_params(2),
    )(glu, glu, cin, cin, cw, cb, lg, lb, pw, ps)


def _merge_kernel(x_ref, oa_ref, ob_ref, oc_ref, od_ref, gpre_ref, gpost_ref, wg_ref, wb_ref, wo_ref, o_ref):
    x = x_ref[0]
    hb = _rms(x, gpre_ref[...]).astype(BF16)
    mix = jnp.zeros((TM, D_MODEL), F32)
    for n, br_ref in enumerate((oa_ref, ob_ref, oc_ref, od_ref)):
        gate = jax.nn.sigmoid(_dot(hb, wg_ref[:, n * D_MODEL:(n + 1) * D_MODEL]))
        mix += gate * _dot(br_ref[0].astype(BF16), wb_ref[n])
    o_ref[0] = x + _rms(_dot(mix.astype(BF16), wo_ref[...]), gpost_ref[...])


def _merge(x, oa, ob, oc, od, gpre, gpost, wg, wb, wo):
    b, s, d = x.shape
    xs = pl.BlockSpec((1, TM, d), lambda i, j: (i, j, 0))
    br = pl.BlockSpec((1, TM, BRANCH_W), lambda i, j: (i, j, 0))
    return pl.pallas_call(
        _merge_kernel,
        grid=(b, s // TM),
        in_specs=[xs, br, br, br, br, _const_spec(gpre.shape), _const_spec(gpost.shape),
                  _const_spec(wg.shape), _const_spec(wb.shape), _const_spec(wo.shape)],
        out_specs=xs,
        out_shape=jax.ShapeDtypeStruct(x.shape, F32),
        compiler_params=_params(2),
    )(x, oa, ob, oc, od, gpre, gpost, wg, wb, wo)


def _ffn_kernel(x_ref, xprev_ref, gpre_ref, gpost_ref, wu_ref, cw_ref, cb_ref, wd_ref, o_ref):
    x = x_ref[0]
    first = pl.program_id(1) == 0
    xe = jnp.concatenate([jnp.where(first, 0.0, xprev_ref[0]), x], axis=0)
    hb = _rms(xe, gpre_ref[...]).astype(BF16)

    def up(c):
        return tuple(_dot(hb, wu_ref[:, col:col + FFN_CHUNK]) for col in (c * FFN_CHUNK, D_FF + c * FFN_CHUNK))

    def conv(u, col):
        out = cb_ref[:, col:col + FFN_CHUNK]
        for k in range(FFN_CONV_WIDTH):
            lo = FFN_HALO - (FFN_CONV_WIDTH - 1) + k
            out = out + cw_ref[k:k + 1, col:col + FFN_CHUNK] * u[lo:lo + TM]
        return out

    def down(c, act):
        return _dot(act, wd_ref[c * FFN_CHUNK:(c + 1) * FFN_CHUNK, :])

    n_chunks = D_FF // FFN_CHUNK
    f = jnp.zeros((TM, D_MODEL), F32)
    nxt, act = up(0), None
    for c in range(n_chunks):
        ug, uv = nxt
        if c + 1 < n_chunks:
            nxt = up(c + 1)
        if act is not None:
            f += down(c - 1, act)
        act = (jax.nn.gelu(conv(ug, c * FFN_CHUNK), approximate=True) * conv(uv, D_FF + c * FFN_CHUNK)).astype(BF16)
    f += down(n_chunks - 1, act)
    o_ref[0] = x + _rms(f, gpost_ref[...])


def _ffn(x, gpre, gpost, wu, cw, cb, wd):
    b, s, d = x.shape
    xs = pl.BlockSpec((1, TM, d), lambda i, j: (i, j, 0))
    prev = pl.BlockSpec((1, FFN_HALO, d), lambda i, j: (i, jnp.maximum(j * (TM // FFN_HALO) - 1, 0), 0))
    return pl.pallas_call(
        _ffn_kernel,
        grid=(b, s // TM),
        in_specs=[xs, prev, _const_spec(gpre.shape), _const_spec(gpost.shape), _const_spec(wu.shape),
                  _const_spec(cw.shape), _const_spec(cb.shape), _const_spec(wd.shape)],
        out_specs=xs,
        out_shape=jax.ShapeDtypeStruct(x.shape, F32),
        compiler_params=_params(2),
    )(x, x, gpre, gpost, wu, cw, cb, wd)


def _rope_tables(seq):
    inv = 1.0 / (ROPE_THETA ** (jnp.arange(0, HEAD_DIM, 2, dtype=F32) / HEAD_DIM))
    ang = jnp.arange(seq, dtype=F32)[:, None] * inv[None, :]
    cos, sin = jnp.cos(ang), jnp.sin(ang)
    reps = LANES // HEAD_DIM
    return jnp.tile(jnp.concatenate([cos, cos], -1), (1, reps)), jnp.tile(jnp.concatenate([-sin, sin], -1), (1, reps))


def kernel(x, norm_mix_pre, norm_mix_post, norm_ffn_pre, norm_ffn_post, w_in, nsa_cmp_pos, nsa_cmp_w1, nsa_cmp_w2, swa_sinks, conv_w, conv_b, conv_ln_g, conv_ln_b, pool_w, pool_scale, w_branch, w_gate, w_o, ffn_w_up, ffn_conv_w, ffn_conv_b, ffn_w_down):
    bsz, seq, d = x.shape
    assert d == D_MODEL and seq % TM == 0 and TQ == TK and TM % TK == 0 and SWA_TQ == SWA_WINDOW and TQ % SWA_TQ == 0
    assert seq // SEL_LEN <= LANES and (seq // SEL_LEN) % SUBLANES == 0 and seq // CMP_STRIDE <= LANES
    depth = w_in.shape[0]
    cos, sin = _rope_tables(seq)
    perm, v_cols = _in_col_permutation()
    row = lambda v: v.reshape(1, -1)
    for l in range(depth):
        w_pad = jnp.concatenate([w_in[l], jnp.zeros((d, 1), w_in.dtype)], axis=1)
        w_perm = jnp.take(w_pad, jnp.asarray(np.where(perm < 0, w_in.shape[2], perm)), axis=1).astype(BF16)
        wvt = jnp.take(w_in[l], jnp.asarray(v_cols), axis=1).T.astype(BF16)
        q_all, k_a, k_d, cmp_raw, vt_a, vt_d, glu, cin, gates = _proj(x, row(norm_mix_pre[l]), w_perm, wvt, cos, sin)

        chunks = cmp_raw.reshape(bsz, 2 * N_KV, seq // CMP_STRIDE, CMP_STRIDE * HEAD_DIM)
        w2 = nsa_cmp_w2[l].astype(BF16)
        kc, kct = _compress(chunks, nsa_cmp_pos[l].reshape(2, 1, CMP_LEN * HEAD_DIM),
                            nsa_cmp_w1[l].astype(BF16), w2, jnp.swapaxes(w2, 1, 2))
        o_a = _nsa(q_all, kc, kct, k_a, vt_a, gates)
        sinks = jnp.broadcast_to(swa_sinks[l].reshape(N_KV, N_REP, 1), (N_KV, N_REP, SWA_TQ))
        o_d = _swa(q_all, k_d, vt_d, sinks)
        o_b, o_c = _convpool(glu, cin, conv_w[l], row(conv_b[l]), row(conv_ln_g[l]), row(conv_ln_b[l]),
                             pool_w[l].astype(BF16), row(pool_scale[l]))
        x = _merge(x, o_a, o_b, o_c, o_d, row(norm_mix_pre[l]), row(norm_mix_post[l]),
                   w_gate[l].astype(BF16), w_branch[l].astype(BF16), w_o[l].astype(BF16))
        x = _ffn(x, row(norm_ffn_pre[l]), row(norm_ffn_post[l]), ffn_w_up[l].astype(BF16),
                 ffn_conv_w[l], row(ffn_conv_b[l]), ffn_w_down[l].astype(BF16))
    return x
```

```python
import numpy as np
import jax
import jax.numpy as jnp
from jax import lax
from jax.experimental import pallas as pl
from jax.experimental.pallas import tpu as pltpu

F32 = jnp.float32
BF16 = jnp.bfloat16

D_MODEL = 1024
HEAD_DIM = 64
HALF = HEAD_DIM // 2
ROPE_THETA = 10000.0
NORM_EPS = 1e-6
NEG = -1e30
M_INIT = -1e29
BRANCH_W = D_MODEL // 2
N_BRANCH = 4
N_HEADS = BRANCH_W // HEAD_DIM
N_KV = 2
N_REP = N_HEADS // N_KV
CMP_LEN = 32
CMP_STRIDE = 16
CMP_HIDDEN = 256
SEL_LEN = 64
SEL_SHIFT = 6
SEL_TOPK = 16
NSA_WINDOW = 512
FORCE_BONUS = 1e3
SWA_WINDOW = 128
CONV_WIDTH = 31
POOL_WINDOWS = (2, 4, 8, 16)
POOL_GROUP_CH = BRANCH_W // len(POOL_WINDOWS)
D_FF = ((8 * D_MODEL // 3) + 127) // 128 * 128
FFN_CONV_WIDTH = 3
IN_SIZES = (BRANCH_W, 3 * 2 * N_KV * HEAD_DIM, 3 * N_HEADS, BRANCH_W, 2 * N_KV * HEAD_DIM, 2 * BRANCH_W, BRANCH_W)

LANES = 128
SUBLANES = 8
TM = 256
TQ = 256
TK = 256
QL = N_REP * TQ
SWA_TQ = 128
SWA_ROWS = 512
ONES_ROWS = 16
LOG2E = 1.4426950408889634
HALO = 32
FFN_HALO = 8
FFN_CHUNK = 256
DOWN_GROUP = 3
VMEM_LIMIT = 56 * 1024 * 1024

ROPE_COLS = 2 * BRANCH_W + 3 * LANES
COL_CMP_K = ROPE_COLS
COL_CMP_V = COL_CMP_K + LANES
COL_GATE = COL_CMP_V + LANES
COL_B_IN = COL_GATE + N_KV * LANES
COL_C_IN = COL_B_IN + 2 * BRANCH_W
N_COLS = COL_C_IN + BRANCH_W
MXU_COLS = 256
N_VT = 3 * N_KV


def _in_col_permutation():
    off = np.cumsum((0,) + IN_SIZES)
    a_q, a_kv, a_gate, d_q, d_kv, b_in, c_in = (np.arange(off[i], off[i + 1]) for i in range(7))
    seg = lambda br, kv: a_kv[(br * 2 + kv) * LANES:(br * 2 + kv + 1) * LANES]
    gate = np.full((N_KV, LANES), -1, np.int64)
    for g in range(N_KV):
        for br in range(3):
            for r in range(N_REP):
                gate[g, br * N_REP + r] = a_gate[br * N_HEADS + g * N_REP + r]
    cols = np.concatenate([a_q, d_q, seg(1, 0), seg(2, 0), d_kv[:LANES],
                           seg(0, 0), seg(0, 1), gate.reshape(-1), b_in, c_in])
    assert cols.shape[0] == N_COLS
    return cols, np.concatenate([seg(1, 1), seg(2, 1), d_kv[LANES:]])


def _const_spec(shape):
    return pl.BlockSpec(shape, lambda *_: (0,) * len(shape), pipeline_mode=pl.Buffered(1))


def _params(n_grid):
    return pltpu.CompilerParams(dimension_semantics=("parallel",) * n_grid, vmem_limit_bytes=VMEM_LIMIT)


def _rms(x, g):
    return x * lax.rsqrt(jnp.mean(x * x, axis=-1, keepdims=True) + NORM_EPS) * g


def _dot(a, b):
    return jnp.dot(a, b, preferred_element_type=F32)


def _dot_t(a, b):
    return lax.dot_general(a, b, (((1,), (1,)), ((), ())), preferred_element_type=F32)


def _proj_kernel(x_ref, g_ref, w_ref, wvt_ref, cos_ref, sin_ref,
                 q_ref, ks_ref, kw_ref, kd_ref, cmp_ref, vta_ref, vtd_ref, glu_ref, cin_ref, gate_ref):
    hb = _rms(x_ref[0], g_ref[...]).astype(BF16)
    cos, sin = cos_ref[...], sin_ref[...]
    first_half = (lax.broadcasted_iota(jnp.int32, (TM, LANES), 1) & (HEAD_DIM - 1)) < HALF

    def mm(col, width=LANES):
        return _dot(hb, w_ref[:, col:col + width])

    def rope(z):
        partner = jnp.where(first_half, pltpu.roll(z, LANES - HALF, 1), pltpu.roll(z, HALF, 1))
        return z * cos + partner * sin

    def put_heads(ref, first, z):
        ref[0, first] = z[:, :HEAD_DIM].astype(ref.dtype)
        ref[0, first + 1] = z[:, HEAD_DIM:].astype(ref.dtype)

    scale = HEAD_DIM ** -0.5 * LOG2E

    def nsa_q(c):
        def put(z):
            put_heads(q_ref, 2 * c, z * scale)
            put_heads(q_ref, N_HEADS + 2 * c, rope(z * scale))
        return put

    def gate(g):
        def put(z):
            gate_ref[0, :, g * LANES:(g + 1) * LANES] = jax.nn.sigmoid(z)
        return put

    segments = [nsa_q(c) for c in range(BRANCH_W // LANES)]
    segments += [lambda z, c=c: put_heads(q_ref, 2 * N_HEADS + 2 * c, rope(z * scale))
                 for c in range(BRANCH_W // LANES)]
    def slc_k(z):
        lane = lax.broadcasted_iota(jnp.int32, (TM, LANES), 1)
        blk = (pl.program_id(1) * TM + lax.broadcasted_iota(jnp.int32, (TM, LANES), 0)) >> SEL_SHIFT
        onehot = jnp.where(lane - HEAD_DIM == blk, 1.0, 0.0)
        zr = rope(z)
        ks_ref[0, 0] = jnp.where(lane < HEAD_DIM, zr, onehot).astype(BF16)
        ks_ref[0, 1] = jnp.where(lane < HEAD_DIM, pltpu.roll(zr, HEAD_DIM, 1), onehot).astype(BF16)

    segments += [slc_k,
                 lambda z: put_heads(kw_ref, 0, rope(z)),
                 lambda z: put_heads(kd_ref, 0, rope(z)),
                 lambda z: put_heads(cmp_ref, 0, z),
                 lambda z: put_heads(cmp_ref, 2, z)]
    segments += [gate(g) for g in range(N_KV)]
    per_dot = MXU_COLS // LANES
    for first in range(0, len(segments), per_dot):
        group = segments[first:first + per_dot]
        z = mm(first * LANES, len(group) * LANES)
        for i, put in enumerate(group):
            put(z[:, i * LANES:(i + 1) * LANES])
    vt = _dot_t(wvt_ref[...], hb).astype(BF16)
    for n in range(2 * N_KV):
        for c in range(TM // TK):
            vta_ref[0, n, c] = vt[n * HEAD_DIM:(n + 1) * HEAD_DIM, c * TK:(c + 1) * TK]
    for n in range(N_KV):
        rows = slice((2 * N_KV + n) * HEAD_DIM, (2 * N_KV + n + 1) * HEAD_DIM)
        for c in range(TM // SWA_TQ):
            vtd_ref[0, n, c] = vt[rows, c * SWA_TQ:(c + 1) * SWA_TQ]
    glu_ref[0] = mm(COL_B_IN, BRANCH_W) * jax.nn.sigmoid(mm(COL_B_IN + BRANCH_W, BRANCH_W))
    cin_ref[0] = mm(COL_C_IN, BRANCH_W)


def _proj(x, g, w, wvt, cos, sin):
    b, s, d = x.shape
    heads = lambda n, dt: (jax.ShapeDtypeStruct((b, n, s, HEAD_DIM), dt),
                           pl.BlockSpec((1, n, TM, HEAD_DIM), lambda i, j: (i, 0, j, 0)))
    rows = lambda n: (jax.ShapeDtypeStruct((b, s, n), F32), pl.BlockSpec((1, TM, n), lambda i, j: (i, j, 0)))
    vts = lambda n, t: (jax.ShapeDtypeStruct((b, n, s // t, HEAD_DIM, t), BF16),
                        pl.BlockSpec((1, n, TM // t, HEAD_DIM, t), lambda i, j: (i, 0, j, 0, 0)))
    k_aug = (jax.ShapeDtypeStruct((b, N_KV, s, LANES), BF16),
             pl.BlockSpec((1, N_KV, TM, LANES), lambda i, j: (i, 0, j, 0)))
    outs = [heads(3 * N_HEADS, BF16), k_aug, heads(N_KV, BF16), heads(N_KV, BF16), heads(2 * N_KV, F32),
            vts(2 * N_KV, TK), vts(N_KV, SWA_TQ), rows(BRANCH_W), rows(BRANCH_W), rows(N_KV * LANES)]
    return pl.pallas_call(
        _proj_kernel,
        grid=(b, s // TM),
        in_specs=[pl.BlockSpec((1, TM, d), lambda i, j: (i, j, 0)),
                  _const_spec((1, d)), _const_spec(w.shape), _const_spec(wvt.shape),
                  pl.BlockSpec((TM, LANES), lambda i, j: (j, 0)),
                  pl.BlockSpec((TM, LANES), lambda i, j: (j, 0))],
        out_specs=[o[1] for o in outs],
        out_shape=[o[0] for o in outs],
        compiler_params=_params(2),
    )(x, g, w, wvt, cos, sin)


def _compress_kernel(c_ref, pos_ref, w1_ref, w2_ref, w2t_ref, o_ref, ot_ref):
    c = c_ref[0, 0]
    n_chunk, half = c.shape
    pos = pos_ref[0]
    top = _dot((c + pos[:, :half]).astype(BF16), w1_ref[0, :half])
    bot = _dot((c + pos[:, half:]).astype(BF16), w1_ref[0, half:])
    hid = top + pltpu.roll(bot, n_chunk - 1, 0)
    act = jax.nn.gelu(hid, approximate=True).astype(BF16)
    out = _dot(act, w2_ref[0])
    row = lax.broadcasted_iota(jnp.int32, out.shape, 0)
    o_ref[0, 0] = jnp.where(row < n_chunk - 1, out, 0.0)
    out_t = _dot_t(w2t_ref[0], act)
    col = lax.broadcasted_iota(jnp.int32, out_t.shape, 1)
    ot_ref[0, 0] = jnp.where(col < n_chunk - 1, out_t, 0.0)


def _compress(chunks, pos, w1, w2, w2t):
    b, n, n_chunk, width = chunks.shape
    return pl.pallas_call(
        _compress_kernel,
        grid=(b, n),
        in_specs=[pl.BlockSpec((1, 1, n_chunk, width), lambda i, j: (i, j, 0, 0)),
                  pl.BlockSpec((1, 1, 2 * width), lambda i, j: (j // N_KV, 0, 0)),
                  pl.BlockSpec((1, 2 * width, CMP_HIDDEN), lambda i, j: (j // N_KV, 0, 0)),
                  pl.BlockSpec((1, CMP_HIDDEN, HEAD_DIM), lambda i, j: (j // N_KV, 0, 0)),
                  pl.BlockSpec((1, HEAD_DIM, CMP_HIDDEN), lambda i, j: (j // N_KV, 0, 0))],
        out_specs=[pl.BlockSpec((1, 1, n_chunk, HEAD_DIM), lambda i, j: (i, j, 0, 0)),
                   pl.BlockSpec((1, 1, HEAD_DIM, n_chunk), lambda i, j: (i, j, 0, 0))],
        out_shape=[jax.ShapeDtypeStruct((b, n, n_chunk, HEAD_DIM), F32),
                   jax.ShapeDtypeStruct((b, n, HEAD_DIM, n_chunk), F32)],
        compiler_params=_params(2),
    )(chunks, pos, w1, w2, w2t)


def _heads_on_lanes(a):
    return jnp.concatenate([a] * N_REP, axis=1)


def _with_ones(vt):
    return jnp.concatenate([vt, jnp.ones((ONES_ROWS, vt.shape[1]), vt.dtype)], axis=0)


def _scores(k_ref, q_refs, kt):
    k = k_ref[0, 0, pl.ds(pl.multiple_of(kt * TK, TK), TK), :]
    return tuple(_dot_t(k, q[...]) for q in q_refs)


def _absorb(k_ref, vt_ref, q_refs, tiles, state):
    ss = [_scores(k_ref, q_refs, kt) for kt, _ in tiles]
    ss = [s if bias is None else [x + bias for x in s] for s, (_, bias) in zip(ss, tiles)]
    ms = []
    for r, (m, _) in enumerate(state):
        for s in ss:
            m = jnp.maximum(m, jnp.max(s[r], axis=0, keepdims=True))
        ms.append(m)
    ps = [jnp.concatenate([jnp.exp2(s[r] - ms[r]).astype(BF16) for s in ss], axis=0) for r in range(N_REP)]
    vt1 = jnp.concatenate([_with_ones(vt_ref[0, 0, kt]) for kt, _ in tiles], axis=1)
    return tuple((m_new, jnp.exp2(m - m_new) * acc + _dot(vt1, p)) for p, m_new, (m, acc) in zip(ps, ms, state))


def _online_init():
    return tuple((jnp.full((1, TQ), M_INIT, F32), jnp.zeros((HEAD_DIM + ONES_ROWS, TQ), F32)) for _ in range(N_REP))


def _normed(state):
    return [acc[:HEAD_DIM] * (1.0 / acc[HEAD_DIM:HEAD_DIM + 1]) for _, acc in state]


def _band_attention(q_refs, k_ref, vt_ref, qi, window):
    tq = qi * TQ + lax.broadcasted_iota(jnp.int32, (TK, TQ), 1)
    row = lax.broadcasted_iota(jnp.int32, (TK, TQ), 0)
    tiles = []
    for d in range(window // TK + 1):
        key = (qi - d) * TK + row
        tiles.append((jnp.maximum(qi - d, 0), jnp.where((key <= tq) & (key > tq - window) & (key >= 0), 0.0, NEG)))
    return _absorb(k_ref, vt_ref, q_refs, tiles, _online_init())


def _heads_to_rows(o_t):
    tq = o_t.shape[1] // N_REP
    return jnp.concatenate([o_t[:, r * tq:(r + 1) * tq] for r in range(N_REP)], axis=0).T


def _nsa_kernel(q_ref, qr_ref, kc_ref, vct_ref, ks_ref, vst_ref, kw_ref, vwt_ref, gate_ref, o_ref, qaug_ref):
    qi = pl.program_id(2)
    q = q_ref[0].reshape(QL, HEAD_DIM)
    n_chunk = kc_ref.shape[2]
    n_slc = ks_ref.shape[2] // SEL_LEN

    blk = lax.broadcasted_iota(jnp.int32, (n_chunk, TQ), 0)
    tqc = qi * TQ + lax.broadcasted_iota(jnp.int32, (n_chunk, TQ), 1)
    vis = (blk * CMP_STRIDE + CMP_LEN - 1 <= tqc) & (blk < n_chunk - 1)
    s = _dot_t(kc_ref[0, 0].astype(BF16), q) + _heads_on_lanes(jnp.where(vis, 0.0, NEG))
    e = jnp.exp2(s - jnp.maximum(jnp.max(s, axis=0, keepdims=True), M_INIT))
    p = e * (1.0 / jnp.maximum(jnp.sum(e, axis=0, keepdims=True), 1e-30))
    o_cmp = _dot(vct_ref[0, 0].astype(BF16), p.astype(BF16))

    p_sum = p[:, :TQ]
    for r in range(1, N_REP):
        p_sum = p_sum + p[:, r * TQ:(r + 1) * TQ]
    oj = lax.broadcasted_iota(jnp.int32, (n_slc, n_chunk), 0)
    oi = lax.broadcasted_iota(jnp.int32, (n_slc, n_chunk), 1)
    overlap = ((oi * CMP_STRIDE < (oj + 1) * SEL_LEN) & (oi * CMP_STRIDE + CMP_LEN > oj * SEL_LEN)).astype(F32)
    imp = jnp.dot(overlap, p_sum, preferred_element_type=F32, precision=lax.Precision.HIGHEST)
    sb = lax.broadcasted_iota(jnp.int32, (n_slc, TQ), 0)
    cur = (qi * TQ + lax.broadcasted_iota(jnp.int32, (n_slc, TQ), 1)) >> SEL_SHIFT
    forced = (sb == 0) | (sb == cur) | (sb == cur - 1)
    score = jnp.where(sb <= cur, imp + jnp.where(forced, FORCE_BONUS, 0.0), -1.0)
    rank = jnp.zeros((n_slc, TQ), F32)
    for i in range(n_slc):
        si = score[i:i + 1, :]
        rank += jnp.where((si > score) | ((si == score) & (sb > i)), 1.0, 0.0)
    sel_bias = jnp.where(rank < min(SEL_TOPK, n_slc), 0.0, NEG)

    pad = jnp.zeros((TQ, HEAD_DIM - n_slc), BF16)
    bias_cols = jnp.concatenate([sel_bias.T.astype(BF16), pad], axis=1)
    for r in range(N_REP):
        qaug_ref[r] = jnp.concatenate([qr_ref[0, r], bias_cols], axis=1)

    qaug = [qaug_ref.at[r] for r in range(N_REP)]

    def slc_pair(i, state):
        return _absorb(ks_ref, vst_ref, qaug, [(2 * i, None), (2 * i + 1, None)], state)

    slc = lax.fori_loop(0, qi // 2, slc_pair, _online_init())
    tq = lax.broadcasted_iota(jnp.int32, (TK, TQ), 1)
    row = lax.broadcasted_iota(jnp.int32, (TK, TQ), 0)
    left_over = jnp.where(qi % 2 == 1, 0.0, NEG) + jnp.zeros((TK, TQ), F32)
    slc = _absorb(ks_ref, vst_ref, qaug,
                  [(jnp.maximum(qi - 1, 0), left_over), (qi, jnp.where(row <= tq, 0.0, NEG))], slc)
    win = _band_attention([qr_ref.at[0, r] for r in range(N_REP)], kw_ref, vwt_ref, qi, NSA_WINDOW)

    gate_t = gate_ref[0].T
    heads = []
    for r, (o_slc, o_win) in enumerate(zip(_normed(slc), _normed(win))):
        g = lambda br: gate_t[br * N_REP + r:br * N_REP + r + 1, :]
        heads.append(g(0) * o_cmp[:, r * TQ:(r + 1) * TQ] + g(1) * o_slc + g(2) * o_win)
    o_ref[0] = jnp.concatenate(heads, axis=0).T


def _nsa(q_all, kc, kct, k_slc, k_win, vt, gates):
    b, _, s, _ = q_all.shape
    n_chunk = kc.shape[2]
    q_spec = lambda first: pl.BlockSpec((1, N_REP, TQ, HEAD_DIM), lambda i, g, j: (i, first + g, j, 0))
    k_spec = lambda width: pl.BlockSpec((1, 1, s, width), lambda i, g, j: (i, g, 0, 0))
    vt_spec = lambda first: pl.BlockSpec((1, 1, s // TK, HEAD_DIM, TK), lambda i, g, j: (i, first + g, 0, 0, 0))
    return pl.pallas_call(
        _nsa_kernel,
        grid=(b, N_KV, s // TQ),
        in_specs=[q_spec(0), q_spec(N_KV),
                  pl.BlockSpec((1, 1, n_chunk, HEAD_DIM), lambda i, g, j: (i, g, 0, 0)),
                  pl.BlockSpec((1, 1, HEAD_DIM, n_chunk), lambda i, g, j: (i, N_KV + g, 0, 0)),
                  k_spec(LANES), vt_spec(0), k_spec(HEAD_DIM), vt_spec(N_KV),
                  pl.BlockSpec((1, TQ, LANES), lambda i, g, j: (i, j, g))],
        out_specs=pl.BlockSpec((1, TQ, N_REP * HEAD_DIM), lambda i, g, j: (i, j, g)),
        out_shape=jax.ShapeDtypeStruct((b, s, BRANCH_W), F32),
        scratch_shapes=[pltpu.VMEM((N_REP, TQ, LANES), BF16)],
        compiler_params=_params(3),
    )(q_all, q_all, kc, kct, k_slc, vt, k_win, vt, gates)


def _swa_kernel(q_ref, k_ref, vt_ref, sink_ref, o_ref):
    sink = jnp.concatenate([sink_ref[0, r:r + 1, :] for r in range(N_REP)], axis=1) * LOG2E
    row = lax.broadcasted_iota(jnp.int32, (2 * SWA_TQ, SWA_TQ), 0)
    col = lax.broadcasted_iota(jnp.int32, (2 * SWA_TQ, SWA_TQ), 1)
    subs = range(SWA_ROWS // SWA_TQ)
    qts = [pl.program_id(2) * len(subs) + sub for sub in subs]
    kts = [jnp.maximum(qt - 1, 0) for qt in qts]
    ss = []
    for sub, qt, kt in zip(subs, qts, kts):
        q = q_ref[0, :, sub * SWA_TQ:(sub + 1) * SWA_TQ, :].reshape(N_REP * SWA_TQ, HEAD_DIM)
        key, tq = kt * SWA_TQ + row, qt * SWA_TQ + col
        bias = _heads_on_lanes(jnp.where((key <= tq) & (key > tq - SWA_WINDOW), 0.0, NEG))
        ss.append(_dot_t(k_ref[0, 0, pl.ds(pl.multiple_of(kt * SWA_TQ, SWA_TQ), 2 * SWA_TQ), :], q) + bias)
    ms = [jnp.maximum(jnp.max(s, axis=0, keepdims=True), sink) for s in ss]
    ps = [jnp.exp2(s - m).astype(BF16) for s, m in zip(ss, ms)]
    accs = [_dot(_with_ones(jnp.concatenate([vt_ref[0, 0, kt], vt_ref[0, 0, kt + 1]], axis=1)), p)
            for kt, p in zip(kts, ps)]
    for sub, acc, m in zip(subs, accs, ms):
        out = acc[:HEAD_DIM] * (1.0 / (acc[HEAD_DIM:HEAD_DIM + 1] + jnp.exp2(sink - m)))
        o_ref[0, sub * SWA_TQ:(sub + 1) * SWA_TQ, :] = _heads_to_rows(out)


def _swa(q_all, k_d, vt, sinks):
    b, _, s, _ = q_all.shape
    return pl.pallas_call(
        _swa_kernel,
        grid=(b, N_KV, s // SWA_ROWS),
        in_specs=[pl.BlockSpec((1, N_REP, SWA_ROWS, HEAD_DIM), lambda i, g, j: (i, 2 * N_KV + g, j, 0)),
                  pl.BlockSpec((1, 1, s, HEAD_DIM), lambda i, g, j: (i, g, 0, 0)),
                  pl.BlockSpec((1, 1, s // SWA_TQ, HEAD_DIM, SWA_TQ), lambda i, g, j: (i, g, 0, 0, 0)),
                  pl.BlockSpec((1, N_REP, SWA_TQ), lambda i, g, j: (g, 0, 0))],
        out_specs=pl.BlockSpec((1, SWA_ROWS, N_REP * HEAD_DIM), lambda i, g, j: (i, j, g)),
        out_shape=jax.ShapeDtypeStruct((b, s, BRANCH_W), F32),
        compiler_params=_params(3),
    )(q_all, k_d, vt, sinks)


def _convpool_kernel(glu_ref, glu_prev_ref, cin_ref, cin_prev_ref, cw_ref, cb_ref, lg_ref, lb_ref, pw_ref, ps_ref,
                     ob_ref, oc_ref, gext, gshift, cext):
    first = pl.program_id(1) == 0
    gext[:HALO] = jnp.where(first, 0.0, glu_prev_ref[0])
    gext[HALO:] = glu_ref[0]
    cext[:HALO] = jnp.where(first, 0.0, cin_prev_ref[0])
    cext[HALO:] = cin_ref[0]
    n_shift = HALO + TM - SUBLANES
    for ph in range(1, SUBLANES):
        gshift[ph - 1, :n_shift] = gext[pl.ds(ph, n_shift), :]

    acc = jnp.zeros((TM, BRANCH_W), F32) + cb_ref[...]
    for k in range(CONV_WIDTH):
        base, ph = divmod(HALO - (CONV_WIDTH - 1) + k, SUBLANES)
        rows = pl.ds(base * SUBLANES, TM)
        acc += cw_ref[k:k + 1, :] * (gext[rows, :] if ph == 0 else gshift[ph - 1, rows, :])
    mu = jnp.mean(acc, axis=-1, keepdims=True)
    cen = acc - mu
    y = cen * lax.rsqrt(jnp.mean(cen * cen, axis=-1, keepdims=True) + NORM_EPS) * lg_ref[...] + lb_ref[...]
    ob_ref[0] = y * jax.nn.sigmoid(y)

    t = pl.program_id(1) * TM + lax.broadcasted_iota(jnp.int32, (TM, 1), 0)
    for g, win in enumerate(POOL_WINDOWS):
        lanes = pl.ds(g * POOL_GROUP_CH, POOL_GROUP_CH)
        tot = cext[pl.ds(HALO, TM), lanes]
        for d in range(1, win):
            tot += cext[pl.ds(HALO - d, TM), lanes]
        pooled = tot / jnp.minimum(t + 1, win).astype(F32) - cext[pl.ds(HALO, TM), lanes]
        oc_ref[0, :, lanes] = _dot(pooled.astype(BF16), pw_ref[g]) * ps_ref[:, lanes]


def _convpool(glu, cin, cw, cb, lg, lb, pw, ps):
    b, s, w = glu.shape
    cur = pl.BlockSpec((1, TM, w), lambda i, j: (i, j, 0))
    prev = pl.BlockSpec((1, HALO, w), lambda i, j: (i, jnp.maximum(j * (TM // HALO) - 1, 0), 0))
    return pl.pallas_call(
        _convpool_kernel,
        grid=(b, s // TM),
        in_specs=[cur, prev, cur, prev, _const_spec(cw.shape), _const_spec(cb.shape), _const_spec(lg.shape),
                  _const_spec(lb.shape), _const_spec(pw.shape), _const_spec(ps.shape)],
        out_specs=[cur, cur],
        out_shape=[jax.ShapeDtypeStruct((b, s, w), F32)] * 2,
        scratch_shapes=[pltpu.VMEM((HALO + TM, w), F32), pltpu.VMEM((SUBLANES - 1, HALO + TM, w), F32),
                        pltpu.VMEM((HALO + TM, w), F32)],
        compiler_params=_params(2),
    )(glu, glu, cin, cin, cw, cb, lg, lb, pw, ps)


def _merge_kernel(x_ref, oa_ref, ob_ref, oc_ref, od_ref, gpre_ref, gpost_ref, wg_ref, wb_ref, wo_ref, o_ref):
    x = x_ref[0]
    hb = _rms(x, gpre_ref[...]).astype(BF16)
    mix = jnp.zeros((TM, D_MODEL), F32)
    for n, br_ref in enumerate((oa_ref, ob_ref, oc_ref, od_ref)):
        gate = jax.nn.sigmoid(_dot(hb, wg_ref[:, n * D_MODEL:(n + 1) * D_MODEL]))
        mix += gate * _dot(br_ref[0].astype(BF16), wb_ref[n])
    o_ref[0] = x + _rms(_dot(mix.astype(BF16), wo_ref[...]), gpost_ref[...])


def _merge(x, oa, ob, oc, od, gpre, gpost, wg, wb, wo):
    b, s, d = x.shape
    xs = pl.BlockSpec((1, TM, d), lambda i, j: (i, j, 0))
    br = pl.BlockSpec((1, TM, BRANCH_W), lambda i, j: (i, j, 0))
    return pl.pallas_call(
        _merge_kernel,
        grid=(b, s // TM),
        in_specs=[xs, br, br, br, br, _const_spec(gpre.shape), _const_spec(gpost.shape),
                  _const_spec(wg.shape), _const_spec(wb.shape), _const_spec(wo.shape)],
        out_specs=xs,
        out_shape=jax.ShapeDtypeStruct(x.shape, F32),
        compiler_params=_params(2),
    )(x, oa, ob, oc, od, gpre, gpost, wg, wb, wo)


def _ffn_kernel(x_ref, xprev_ref, gpre_ref, gpost_ref, wu_ref, cw_ref, cb_ref, wd_ref, o_ref):
    x = x_ref[0]
    first = pl.program_id(1) == 0
    xe = jnp.concatenate([jnp.where(first, 0.0, xprev_ref[0]), x], axis=0)
    hb = _rms(xe, gpre_ref[...]).astype(BF16)

    def up(c):
        return tuple(_dot(hb, wu_ref[:, col:col + FFN_CHUNK]) for col in (c * FFN_CHUNK, D_FF + c * FFN_CHUNK))

    def conv(u, col):
        out = cb_ref[:, col:col + FFN_CHUNK]
        for k in range(FFN_CONV_WIDTH):
            lo = FFN_HALO - (FFN_CONV_WIDTH - 1) + k
            out = out + cw_ref[k:k + 1, col:col + FFN_CHUNK] * u[lo:lo + TM]
        return out

    def down(c, act):
        return _dot(act, wd_ref[c * FFN_CHUNK:(c + 1) * FFN_CHUNK, :])

    n_chunks = D_FF // FFN_CHUNK
    f = jnp.zeros((TM, D_MODEL), F32)
    nxt, acts = up(0), []
    for c in range(n_chunks):
        ug, uv = nxt
        if c + 1 < n_chunks:
            nxt = up(c + 1)
        if len(acts) == DOWN_GROUP:
            f += _dot(jnp.concatenate(acts, axis=1), wd_ref[(c - DOWN_GROUP) * FFN_CHUNK:c * FFN_CHUNK, :])
            acts = []
        acts.append((jax.nn.gelu(conv(ug, c * FFN_CHUNK), approximate=True)
                     * conv(uv, D_FF + c * FFN_CHUNK)).astype(BF16))
    f += _dot(jnp.concatenate(acts, axis=1), wd_ref[(n_chunks - len(acts)) * FFN_CHUNK:, :])
    o_ref[0] = x + _rms(f, gpost_ref[...])


def _ffn(x, gpre, gpost, wu, cw, cb, wd):
    b, s, d = x.shape
    xs = pl.BlockSpec((1, TM, d), lambda i, j: (i, j, 0))
    prev = pl.BlockSpec((1, FFN_HALO, d), lambda i, j: (i, jnp.maximum(j * (TM // FFN_HALO) - 1, 0), 0))
    return pl.pallas_call(
        _ffn_kernel,
        grid=(b, s // TM),
        in_specs=[xs, prev, _const_spec(gpre.shape), _const_spec(gpost.shape), _const_spec(wu.shape),
                  _const_spec(cw.shape), _const_spec(cb.shape), _const_spec(wd.shape)],
        out_specs=xs,
        out_shape=jax.ShapeDtypeStruct(x.shape, F32),
        compiler_params=_params(2),
    )(x, x, gpre, gpost, wu, cw, cb, wd)


def _rope_tables(seq):
    inv = 1.0 / (ROPE_THETA ** (jnp.arange(0, HEAD_DIM, 2, dtype=F32) / HEAD_DIM))
    ang = jnp.arange(seq, dtype=F32)[:, None] * inv[None, :]
    cos, sin = jnp.cos(ang), jnp.sin(ang)
    reps = LANES // HEAD_DIM
    return jnp.tile(jnp.concatenate([cos, cos], -1), (1, reps)), jnp.tile(jnp.concatenate([-sin, sin], -1), (1, reps))


def kernel(x, norm_mix_pre, norm_mix_post, norm_ffn_pre, norm_ffn_post, w_in, nsa_cmp_pos, nsa_cmp_w1, nsa_cmp_w2, swa_sinks, conv_w, conv_b, conv_ln_g, conv_ln_b, pool_w, pool_scale, w_branch, w_gate, w_o, ffn_w_up, ffn_conv_w, ffn_conv_b, ffn_w_down):
    bsz, seq, d = x.shape
    assert d == D_MODEL and seq % TM == 0 and TQ == TK and TM % TK == 0 and SWA_TQ == SWA_WINDOW
    assert seq % SWA_ROWS == 0 and SWA_ROWS % SWA_TQ == 0 and D_FF % FFN_CHUNK == 0
    assert seq // SEL_LEN <= HEAD_DIM and (seq // SEL_LEN) % SUBLANES == 0 and seq // CMP_STRIDE <= LANES
    depth = w_in.shape[0]
    cos, sin = _rope_tables(seq)
    perm, v_cols = _in_col_permutation()
    row = lambda v: v.reshape(1, -1)
    for l in range(depth):
        w_pad = jnp.concatenate([w_in[l], jnp.zeros((d, 1), w_in.dtype)], axis=1)
        w_perm = jnp.take(w_pad, jnp.asarray(np.where(perm < 0, w_in.shape[2], perm)), axis=1).astype(BF16)
        wvt = jnp.take(w_in[l], jnp.asarray(v_cols), axis=1).T.astype(BF16)
        q_all, k_slc, k_win, k_d, cmp_raw, vt_a, vt_d, glu, cin, gates = _proj(
            x, row(norm_mix_pre[l]), w_perm, wvt, cos, sin)

        chunks = cmp_raw.reshape(bsz, 2 * N_KV, seq // CMP_STRIDE, CMP_STRIDE * HEAD_DIM)
        w2 = nsa_cmp_w2[l].astype(BF16)
        kc, kct = _compress(chunks, nsa_cmp_pos[l].reshape(2, 1, CMP_LEN * HEAD_DIM),
                            nsa_cmp_w1[l].astype(BF16), w2, jnp.swapaxes(w2, 1, 2))
        o_a = _nsa(q_all, kc, kct, k_slc, k_win, vt_a, gates)
        sinks = jnp.broadcast_to(swa_sinks[l].reshape(N_KV, N_REP, 1), (N_KV, N_REP, SWA_TQ))
        o_d = _swa(q_all, k_d, vt_d, sinks)
        o_b, o_c = _convpool(glu, cin, conv_w[l], row(conv_b[l]), row(conv_ln_g[l]), row(conv_ln_b[l]),
                             pool_w[l].astype(BF16), row(pool_scale[l]))
        x = _merge(x, o_a, o_b, o_c, o_d, row(norm_mix_pre[l]), row(norm_mix_post[l]),
                   w_gate[l].astype(BF16), w_branch[l].astype(BF16), w_o[l].astype(BF16))
        x = _ffn(x, row(norm_ffn_pre[l]), row(norm_ffn_post[l]), ffn_w_up[l].astype(BF16),
                 ffn_conv_w[l], row(ffn_conv_b[l]), ffn_w_down[l].astype(BF16))
    return x
```

```python
import numpy as np
import jax
import jax.numpy as jnp
from jax import lax
from jax.experimental import pallas as pl
from jax.experimental.pallas import tpu as pltpu

F32 = jnp.float32
BF16 = jnp.bfloat16

D_MODEL = 1024
HEAD_DIM = 64
HALF = HEAD_DIM // 2
ROPE_THETA = 10000.0
NORM_EPS = 1e-6
NEG = -1e30
M_INIT = -1e29
BRANCH_W = D_MODEL // 2
N_BRANCH = 4
N_HEADS = BRANCH_W // HEAD_DIM
N_KV = 2
N_REP = N_HEADS // N_KV
CMP_LEN = 32
CMP_STRIDE = 16
CMP_HIDDEN = 256
SEL_LEN = 64
SEL_SHIFT = 6
SEL_TOPK = 16
PER_SEL = SEL_LEN // CMP_STRIDE
RANK_CHAINS = 4
NSA_WINDOW = 512
FORCE_BONUS = 1e3
SWA_WINDOW = 128
CONV_WIDTH = 31
POOL_WINDOWS = (2, 4, 8, 16)
POOL_GROUP_CH = BRANCH_W // len(POOL_WINDOWS)
D_FF = ((8 * D_MODEL // 3) + 127) // 128 * 128
FFN_CONV_WIDTH = 3
IN_SIZES = (BRANCH_W, 3 * 2 * N_KV * HEAD_DIM, 3 * N_HEADS, BRANCH_W, 2 * N_KV * HEAD_DIM, 2 * BRANCH_W, BRANCH_W)

LANES = 128
SUBLANES = 8
TM = 256
TQ = 256
TK = 256
QL = N_REP * TQ
SWA_TQ = 128
SWA_ROWS = 512
ONES_ROWS = 16
LOG2E = 1.4426950408889634
HALO = 32
FFN_HALO = 8
FFN_CHUNK = 256
DOWN_GROUP = 3
VMEM_LIMIT = 56 * 1024 * 1024

ROPE_COLS = 2 * BRANCH_W + 3 * LANES
COL_CMP_K = ROPE_COLS
COL_CMP_V = COL_CMP_K + LANES
COL_GATE = COL_CMP_V + LANES
COL_B_IN = COL_GATE + N_KV * LANES
COL_C_IN = COL_B_IN + 2 * BRANCH_W
N_COLS = COL_C_IN + BRANCH_W
MXU_COLS = 256
N_VT = 3 * N_KV


def _in_col_permutation():
    off = np.cumsum((0,) + IN_SIZES)
    a_q, a_kv, a_gate, d_q, d_kv, b_in, c_in = (np.arange(off[i], off[i + 1]) for i in range(7))
    seg = lambda br, kv: a_kv[(br * 2 + kv) * LANES:(br * 2 + kv + 1) * LANES]
    gate = np.full((N_KV, LANES), -1, np.int64)
    for g in range(N_KV):
        for br in range(3):
            for r in range(N_REP):
                gate[g, br * N_REP + r] = a_gate[br * N_HEADS + g * N_REP + r]
    cols = np.concatenate([a_q, d_q, seg(1, 0), seg(2, 0), d_kv[:LANES],
                           seg(0, 0), seg(0, 1), gate.reshape(-1), b_in, c_in])
    assert cols.shape[0] == N_COLS
    return cols, np.concatenate([seg(1, 1), seg(2, 1), d_kv[LANES:]])


def _const_spec(shape):
    return pl.BlockSpec(shape, lambda *_: (0,) * len(shape), pipeline_mode=pl.Buffered(1))


def _params(n_grid):
    return pltpu.CompilerParams(dimension_semantics=("parallel",) * n_grid, vmem_limit_bytes=VMEM_LIMIT)


def _rms(x, g):
    return x * lax.rsqrt(jnp.mean(x * x, axis=-1, keepdims=True) + NORM_EPS) * g


def _dot(a, b):
    return jnp.dot(a, b, preferred_element_type=F32)


def _dot_t(a, b):
    return lax.dot_general(a, b, (((1,), (1,)), ((), ())), preferred_element_type=F32)


def _proj_kernel(x_ref, g_ref, w_ref, wvt_ref, cos_ref, sin_ref,
                 q_ref, qra_ref, ks_ref, kw_ref, kd_ref, cmp_ref, vta_ref, vtd_ref, glu_ref, cin_ref, gate_ref):
    hb = _rms(x_ref[0], g_ref[...]).astype(BF16)
    cos, sin = cos_ref[...], sin_ref[...]
    first_half = (lax.broadcasted_iota(jnp.int32, (TM, LANES), 1) & (HEAD_DIM - 1)) < HALF

    def mm(col, width=LANES):
        return _dot(hb, w_ref[:, col:col + width])

    def rope(z):
        partner = jnp.where(first_half, pltpu.roll(z, LANES - HALF, 1), pltpu.roll(z, HALF, 1))
        return z * cos + partner * sin

    def put_heads(ref, first, z):
        ref[0, first] = z[:, :HEAD_DIM].astype(ref.dtype)
        ref[0, first + 1] = z[:, HEAD_DIM:].astype(ref.dtype)

    lane = lax.broadcasted_iota(jnp.int32, (TM, LANES), 1)

    def put_wide(ref, first, z, fill=0.0):
        ref[0, first] = jnp.where(lane < HEAD_DIM, z, fill).astype(BF16)
        ref[0, first + 1] = jnp.where(lane < HEAD_DIM, pltpu.roll(z, HEAD_DIM, 1), fill).astype(BF16)

    scale = HEAD_DIM ** -0.5 * LOG2E

    def nsa_q(c):
        def put(z):
            put_heads(q_ref, 2 * c, z * scale)
            put_wide(qra_ref, 2 * c, rope(z * scale))
        return put

    def gate(g):
        def put(z):
            gate_ref[0, :, g * LANES:(g + 1) * LANES] = jax.nn.sigmoid(z)
        return put

    segments = [nsa_q(c) for c in range(BRANCH_W // LANES)]
    segments += [lambda z, c=c: put_heads(q_ref, N_HEADS + 2 * c, rope(z * scale))
                 for c in range(BRANCH_W // LANES)]

    def slc_k(z):
        blk = (pl.program_id(1) * TM + lax.broadcasted_iota(jnp.int32, (TM, LANES), 0)) >> SEL_SHIFT
        put_wide(ks_ref, 0, rope(z), jnp.where(lane - HEAD_DIM == blk, 1.0, 0.0))

    segments += [slc_k,
                 lambda z: put_wide(kw_ref, 0, rope(z)),
                 lambda z: put_heads(kd_ref, 0, rope(z)),
                 lambda z: put_heads(cmp_ref, 0, z),
                 lambda z: put_heads(cmp_ref, 2, z)]
    segments += [gate(g) for g in range(N_KV)]
    per_dot = MXU_COLS // LANES
    for first in range(0, len(segments), per_dot):
        group = segments[first:first + per_dot]
        z = mm(first * LANES, len(group) * LANES)
        for i, put in enumerate(group):
            put(z[:, i * LANES:(i + 1) * LANES])
    vt = _dot_t(wvt_ref[...], hb).astype(BF16)
    for n in range(2 * N_KV):
        for c in range(TM // TK):
            vta_ref[0, n, c] = vt[n * HEAD_DIM:(n + 1) * HEAD_DIM, c * TK:(c + 1) * TK]
    for n in range(N_KV):
        rows = slice((2 * N_KV + n) * HEAD_DIM, (2 * N_KV + n + 1) * HEAD_DIM)
        for c in range(TM // SWA_TQ):
            vtd_ref[0, n, c] = vt[rows, c * SWA_TQ:(c + 1) * SWA_TQ]
    glu_ref[0] = mm(COL_B_IN, BRANCH_W) * jax.nn.sigmoid(mm(COL_B_IN + BRANCH_W, BRANCH_W))
    cin_ref[0] = mm(COL_C_IN, BRANCH_W)


def _proj(x, g, w, wvt, cos, sin):
    b, s, d = x.shape
    heads = lambda n, dt: (jax.ShapeDtypeStruct((b, n, s, HEAD_DIM), dt),
                           pl.BlockSpec((1, n, TM, HEAD_DIM), lambda i, j: (i, 0, j, 0)))
    rows = lambda n: (jax.ShapeDtypeStruct((b, s, n), F32), pl.BlockSpec((1, TM, n), lambda i, j: (i, j, 0)))
    vts = lambda n, t: (jax.ShapeDtypeStruct((b, n, s // t, HEAD_DIM, t), BF16),
                        pl.BlockSpec((1, n, TM // t, HEAD_DIM, t), lambda i, j: (i, 0, j, 0, 0)))
    wide = lambda n: (jax.ShapeDtypeStruct((b, n, s, LANES), BF16),
                      pl.BlockSpec((1, n, TM, LANES), lambda i, j: (i, 0, j, 0)))
    outs = [heads(2 * N_HEADS, BF16), wide(N_HEADS), wide(N_KV), wide(N_KV), heads(N_KV, BF16), heads(2 * N_KV, F32),
            vts(2 * N_KV, TK), vts(N_KV, SWA_TQ), rows(BRANCH_W), rows(BRANCH_W), rows(N_KV * LANES)]
    return pl.pallas_call(
        _proj_kernel,
        grid=(b, s // TM),
        in_specs=[pl.BlockSpec((1, TM, d), lambda i, j: (i, j, 0)),
                  _const_spec((1, d)), _const_spec(w.shape), _const_spec(wvt.shape),
                  pl.BlockSpec((TM, LANES), lambda i, j: (j, 0)),
                  pl.BlockSpec((TM, LANES), lambda i, j: (j, 0))],
        out_specs=[o[1] for o in outs],
        out_shape=[o[0] for o in outs],
        compiler_params=_params(2),
    )(x, g, w, wvt, cos, sin)


def _compress_kernel(c_ref, pos_ref, w1_ref, w2_ref, w2t_ref, o_ref, ot_ref):
    c = c_ref[0, 0]
    n_chunk, half = c.shape
    pos = pos_ref[0]
    top = _dot((c + pos[:, :half]).astype(BF16), w1_ref[0, :half])
    bot = _dot((c + pos[:, half:]).astype(BF16), w1_ref[0, half:])
    hid = top + pltpu.roll(bot, n_chunk - 1, 0)
    act = jax.nn.gelu(hid, approximate=True).astype(BF16)
    out = _dot(act, w2_ref[0])
    row = lax.broadcasted_iota(jnp.int32, out.shape, 0)
    o_ref[0, 0] = jnp.where(row < n_chunk - 1, out, 0.0)
    out_t = _dot_t(w2t_ref[0], act)
    col = lax.broadcasted_iota(jnp.int32, out_t.shape, 1)
    ot_ref[0, 0] = jnp.where(col < n_chunk - 1, out_t, 0.0)


def _compress(chunks, pos, w1, w2, w2t):
    b, n, n_chunk, width = chunks.shape
    return pl.pallas_call(
        _compress_kernel,
        grid=(b, n),
        in_specs=[pl.BlockSpec((1, 1, n_chunk, width), lambda i, j: (i, j, 0, 0)),
                  pl.BlockSpec((1, 1, 2 * width), lambda i, j: (j // N_KV, 0, 0)),
                  pl.BlockSpec((1, 2 * width, CMP_HIDDEN), lambda i, j: (j // N_KV, 0, 0)),
                  pl.BlockSpec((1, CMP_HIDDEN, HEAD_DIM), lambda i, j: (j // N_KV, 0, 0)),
                  pl.BlockSpec((1, HEAD_DIM, CMP_HIDDEN), lambda i, j: (j // N_KV, 0, 0))],
        out_specs=[pl.BlockSpec((1, 1, n_chunk, HEAD_DIM), lambda i, j: (i, j, 0, 0)),
                   pl.BlockSpec((1, 1, HEAD_DIM, n_chunk), lambda i, j: (i, j, 0, 0))],
        out_shape=[jax.ShapeDtypeStruct((b, n, n_chunk, HEAD_DIM), F32),
                   jax.ShapeDtypeStruct((b, n, HEAD_DIM, n_chunk), F32)],
        compiler_params=_params(2),
    )(chunks, pos, w1, w2, w2t)


def _heads_on_lanes(a):
    return jnp.concatenate([a] * N_REP, axis=1)


def _with_ones(vt):
    return jnp.concatenate([vt, jnp.ones((ONES_ROWS, vt.shape[1]), vt.dtype)], axis=0)


def _scores(k_ref, q_refs, kt):
    k = k_ref[0, 0, pl.ds(pl.multiple_of(kt * TK, TK), TK), :]
    return tuple(_dot_t(k, q[...]) for q in q_refs)


def _absorb(k_ref, vt_ref, q_refs, tiles, state):
    ss = [_scores(k_ref, q_refs, kt) for kt, _ in tiles]
    ss = [s if bias is None else [x + bias for x in s] for s, (_, bias) in zip(ss, tiles)]
    ms = []
    for r, (m, _) in enumerate(state):
        for s in ss:
            m = jnp.maximum(m, jnp.max(s[r], axis=0, keepdims=True))
        ms.append(m)
    ps = [jnp.concatenate([jnp.exp2(s[r] - ms[r]).astype(BF16) for s in ss], axis=0) for r in range(N_REP)]
    vt1 = jnp.concatenate([_with_ones(vt_ref[0, 0, kt]) for kt, _ in tiles], axis=1)
    return tuple((m_new, jnp.exp2(m - m_new) * acc + _dot(vt1, p)) for p, m_new, (m, acc) in zip(ps, ms, state))


def _online_init():
    return tuple((jnp.full((1, TQ), M_INIT, F32), jnp.zeros((HEAD_DIM + ONES_ROWS, TQ), F32)) for _ in range(N_REP))


def _normed(state):
    return [acc[:HEAD_DIM] * (1.0 / acc[HEAD_DIM:HEAD_DIM + 1]) for _, acc in state]


def _band_attention(q_refs, k_ref, vt_ref, qi, window):
    tq = qi * TQ + lax.broadcasted_iota(jnp.int32, (TK, TQ), 1)
    row = lax.broadcasted_iota(jnp.int32, (TK, TQ), 0)
    tiles = []
    for d in range(window // TK + 1):
        key = (qi - d) * TK + row
        tiles.append((jnp.maximum(qi - d, 0), jnp.where((key <= tq) & (key > tq - window) & (key >= 0), 0.0, NEG)))
    return _absorb(k_ref, vt_ref, q_refs, tiles, _online_init())


def _heads_to_rows(o_t):
    tq = o_t.shape[1] // N_REP
    return jnp.concatenate([o_t[:, r * tq:(r + 1) * tq] for r in range(N_REP)], axis=0).T


def _nsa_kernel(q_ref, qr_ref, kc_ref, vct_ref, ks_ref, vst_ref, kw_ref, vwt_ref, gate_ref, o_ref,
                qaug_ref, psum_ref):
    qi = pl.program_id(2)
    q = q_ref[0].reshape(QL, HEAD_DIM)
    n_chunk = kc_ref.shape[2]
    n_slc = ks_ref.shape[2] // SEL_LEN

    win = _band_attention([qr_ref.at[0, r] for r in range(N_REP)], kw_ref, vwt_ref, qi, NSA_WINDOW)

    blk = lax.broadcasted_iota(jnp.int32, (n_chunk, TQ), 0)
    tqc = qi * TQ + lax.broadcasted_iota(jnp.int32, (n_chunk, TQ), 1)
    vis = (blk * CMP_STRIDE + CMP_LEN - 1 <= tqc) & (blk < n_chunk - 1)
    s = _dot_t(kc_ref[0, 0].astype(BF16), q) + _heads_on_lanes(jnp.where(vis, 0.0, NEG))
    e = jnp.exp2(s - jnp.maximum(jnp.max(s, axis=0, keepdims=True), M_INIT))
    p = e * (1.0 / jnp.maximum(jnp.sum(e, axis=0, keepdims=True), 1e-30))
    o_cmp = _dot(vct_ref[0, 0].astype(BF16), p.astype(BF16))

    p_sum = p[:, :TQ]
    for r in range(1, N_REP):
        p_sum = p_sum + p[:, r * TQ:(r + 1) * TQ]
    halves = range(TQ // LANES)
    for h in halves:
        psum_ref[h] = p_sum[:, h * LANES:(h + 1) * LANES]
    inside = [jnp.concatenate([psum_ref[h, pl.ds(k, n_slc, stride=PER_SEL), :] for h in halves], axis=1)
              for k in range(PER_SEL)]
    sb = lax.broadcasted_iota(jnp.int32, (n_slc, TQ), 0)
    imp = jnp.where(sb == 0, 0.0, pltpu.roll(inside[-1], 1, 0))
    for part in inside:
        imp = imp + part
    cur = (qi * TQ + lax.broadcasted_iota(jnp.int32, (n_slc, TQ), 1)) >> SEL_SHIFT
    forced = (sb == 0) | (sb == cur) | (sb == cur - 1)
    score = jnp.where(sb <= cur, imp + jnp.where(forced, FORCE_BONUS, 0.0), -1.0)
    ranks = [jnp.zeros((n_slc, TQ), F32) for _ in range(RANK_CHAINS)]
    for i in range(n_slc):
        si = score[i:i + 1, :]
        ranks[i % RANK_CHAINS] += jnp.where((si > score) | ((si == score) & (sb > i)), 1.0, 0.0)
    sel_bias = jnp.where(sum(ranks) < min(SEL_TOPK, n_slc), 0.0, NEG)

    zeros = lambda n: jnp.zeros((n, TQ), F32)
    bias_t = jnp.concatenate([zeros(HEAD_DIM), sel_bias, zeros(LANES - HEAD_DIM - n_slc)], axis=0).T.astype(BF16)
    for r in range(N_REP):
        qaug_ref[r] = qr_ref[0, r] + bias_t

    qaug = [qaug_ref.at[r] for r in range(N_REP)]

    def slc_pair(i, state):
        return _absorb(ks_ref, vst_ref, qaug, [(2 * i, None), (2 * i + 1, None)], state)

    slc = lax.fori_loop(0, qi // 2, slc_pair, _online_init())
    tq = lax.broadcasted_iota(jnp.int32, (TK, TQ), 1)
    row = lax.broadcasted_iota(jnp.int32, (TK, TQ), 0)
    diagonal = (qi, jnp.where(row <= tq, 0.0, NEG))
    slc = lax.cond(qi % 2 == 1,
                   lambda state: _absorb(ks_ref, vst_ref, qaug, [(qi - 1, None), diagonal], state),
                   lambda state: _absorb(ks_ref, vst_ref, qaug, [diagonal], state), slc)

    gate_t = gate_ref[0].T
    heads = []
    for r, (o_slc, o_win) in enumerate(zip(_normed(slc), _normed(win))):
        g = lambda br: gate_t[br * N_REP + r:br * N_REP + r + 1, :]
        heads.append(g(0) * o_cmp[:, r * TQ:(r + 1) * TQ] + g(1) * o_slc + g(2) * o_win)
    o_ref[0] = jnp.concatenate(heads, axis=0).T


def _nsa(q_all, q_rot, kc, kct, k_slc, k_win, vt, gates):
    b, _, s, _ = q_all.shape
    n_chunk = kc.shape[2]
    q_spec = lambda width: pl.BlockSpec((1, N_REP, TQ, width), lambda i, g, j: (i, g, j, 0))
    k_spec = pl.BlockSpec((1, 1, s, LANES), lambda i, g, j: (i, g, 0, 0))
    vt_spec = lambda first: pl.BlockSpec((1, 1, s // TK, HEAD_DIM, TK), lambda i, g, j: (i, first + g, 0, 0, 0))
    return pl.pallas_call(
        _nsa_kernel,
        grid=(b, N_KV, s // TQ),
        in_specs=[q_spec(HEAD_DIM), q_spec(LANES),
                  pl.BlockSpec((1, 1, n_chunk, HEAD_DIM), lambda i, g, j: (i, g, 0, 0)),
                  pl.BlockSpec((1, 1, HEAD_DIM, n_chunk), lambda i, g, j: (i, N_KV + g, 0, 0)),
                  k_spec, vt_spec(0), k_spec, vt_spec(N_KV),
                  pl.BlockSpec((1, TQ, LANES), lambda i, g, j: (i, j, g))],
        out_specs=pl.BlockSpec((1, TQ, N_REP * HEAD_DIM), lambda i, g, j: (i, j, g)),
        out_shape=jax.ShapeDtypeStruct((b, s, BRANCH_W), F32),
        scratch_shapes=[pltpu.VMEM((N_REP, TQ, LANES), BF16), pltpu.VMEM((TQ // LANES, n_chunk, LANES), F32)],
        compiler_params=_params(3),
    )(q_all, q_rot, kc, kct, k_slc, vt, k_win, vt, gates)


def _swa_kernel(q_ref, k_ref, vt_ref, sink_ref, o_ref):
    sink = jnp.concatenate([sink_ref[0, r:r + 1, :] for r in range(N_REP)], axis=1) * LOG2E
    row = lax.broadcasted_iota(jnp.int32, (2 * SWA_TQ, SWA_TQ), 0)
    col = lax.broadcasted_iota(jnp.int32, (2 * SWA_TQ, SWA_TQ), 1)
    subs = range(SWA_ROWS // SWA_TQ)
    qts = [pl.program_id(2) * len(subs) + sub for sub in subs]
    kts = [jnp.maximum(qt - 1, 0) for qt in qts]
    ss = []
    for sub, qt, kt in zip(subs, qts, kts):
        q = q_ref[0, :, sub * SWA_TQ:(sub + 1) * SWA_TQ, :].reshape(N_REP * SWA_TQ, HEAD_DIM)
        key, tq = kt * SWA_TQ + row, qt * SWA_TQ + col
        bias = _heads_on_lanes(jnp.where((key <= tq) & (key > tq - SWA_WINDOW), 0.0, NEG))
        ss.append(_dot_t(k_ref[0, 0, pl.ds(pl.multiple_of(kt * SWA_TQ, SWA_TQ), 2 * SWA_TQ), :], q) + bias)
    ms = [jnp.maximum(jnp.max(s, axis=0, keepdims=True), sink) for s in ss]
    ps = [jnp.exp2(s - m).astype(BF16) for s, m in zip(ss, ms)]
    accs = [_dot(_with_ones(jnp.concatenate([vt_ref[0, 0, kt], vt_ref[0, 0, kt + 1]], axis=1)), p)
            for kt, p in zip(kts, ps)]
    for sub, acc, m in zip(subs, accs, ms):
        out = acc[:HEAD_DIM] * (1.0 / (acc[HEAD_DIM:HEAD_DIM + 1] + jnp.exp2(sink - m)))
        o_ref[0, sub * SWA_TQ:(sub + 1) * SWA_TQ, :] = _heads_to_rows(out)


def _swa(q_all, k_d, vt, sinks):
    b, _, s, _ = q_all.shape
    return pl.pallas_call(
        _swa_kernel,
        grid=(b, N_KV, s // SWA_ROWS),
        in_specs=[pl.BlockSpec((1, N_REP, SWA_ROWS, HEAD_DIM), lambda i, g, j: (i, N_KV + g, j, 0)),
                  pl.BlockSpec((1, 1, s, HEAD_DIM), lambda i, g, j: (i, g, 0, 0)),
                  pl.BlockSpec((1, 1, s // SWA_TQ, HEAD_DIM, SWA_TQ), lambda i, g, j: (i, g, 0, 0, 0)),
                  pl.BlockSpec((1, N_REP, SWA_TQ), lambda i, g, j: (g, 0, 0))],
        out_specs=pl.BlockSpec((1, SWA_ROWS, N_REP * HEAD_DIM), lambda i, g, j: (i, j, g)),
        out_shape=jax.ShapeDtypeStruct((b, s, BRANCH_W), F32),
        compiler_params=_params(3),
    )(q_all, k_d, vt, sinks)


def _mix_kernel(x_ref, oa_ref, od_ref, glu_ref, glu_prev_ref, cin_ref, cin_prev_ref,
                cw_ref, cb_ref, lg_ref, lb_ref, pw_ref, ps_ref, gpre_ref, gpost_ref, wg_ref, wb_ref, wo_ref,
                o_ref, gext, gshift, cext):
    x = x_ref[0]
    hb = _rms(x, gpre_ref[...]).astype(BF16)

    def gated_up(n, branch):
        gate = jax.nn.sigmoid(_dot(hb, wg_ref[:, n * D_MODEL:(n + 1) * D_MODEL]))
        return gate * _dot(branch.astype(BF16), wb_ref[n])

    first = pl.program_id(1) == 0
    gext[:HALO] = jnp.where(first, 0.0, glu_prev_ref[0])
    gext[HALO:] = glu_ref[0]
    cext[:HALO] = jnp.where(first, 0.0, cin_prev_ref[0])
    cext[HALO:] = cin_ref[0]
    n_shift = HALO + TM - SUBLANES
    for ph in range(1, SUBLANES):
        gshift[ph - 1, :n_shift] = gext[pl.ds(ph, n_shift), :]

    mix = gated_up(0, oa_ref[0])

    acc = jnp.zeros((TM, BRANCH_W), F32) + cb_ref[...]
    for k in range(CONV_WIDTH):
        base, ph = divmod(HALO - (CONV_WIDTH - 1) + k, SUBLANES)
        rows = pl.ds(base * SUBLANES, TM)
        acc += cw_ref[k:k + 1, :] * (gext[rows, :] if ph == 0 else gshift[ph - 1, rows, :])
    mix += gated_up(N_BRANCH - 1, od_ref[0])
    mu = jnp.mean(acc, axis=-1, keepdims=True)
    cen = acc - mu
    y = cen * lax.rsqrt(jnp.mean(cen * cen, axis=-1, keepdims=True) + NORM_EPS) * lg_ref[...] + lb_ref[...]
    mix += gated_up(1, y * jax.nn.sigmoid(y))

    t = pl.program_id(1) * TM + lax.broadcasted_iota(jnp.int32, (TM, 1), 0)
    pooled = []
    for g, win in enumerate(POOL_WINDOWS):
        lanes = pl.ds(g * POOL_GROUP_CH, POOL_GROUP_CH)
        tot = cext[pl.ds(HALO, TM), lanes]
        for d in range(1, win):
            tot += cext[pl.ds(HALO - d, TM), lanes]
        mean_minus_token = tot / jnp.minimum(t + 1, win).astype(F32) - cext[pl.ds(HALO, TM), lanes]
        pooled.append(_dot(mean_minus_token.astype(BF16), pw_ref[g]))
    mix += gated_up(2, jnp.concatenate(pooled, axis=1) * ps_ref[...])

    o_ref[0] = x + _rms(_dot(mix.astype(BF16), wo_ref[...]), gpost_ref[...])


def _mix(x, oa, od, glu, cin, cw, cb, lg, lb, pw, ps, gpre, gpost, wg, wb, wo):
    b, s, d = x.shape
    xs = pl.BlockSpec((1, TM, d), lambda i, j: (i, j, 0))
    cur = pl.BlockSpec((1, TM, BRANCH_W), lambda i, j: (i, j, 0))
    prev = pl.BlockSpec((1, HALO, BRANCH_W), lambda i, j: (i, jnp.maximum(j * (TM // HALO) - 1, 0), 0))
    consts = (cw, cb, lg, lb, pw, ps, gpre, gpost, wg, wb, wo)
    return pl.pallas_call(
        _mix_kernel,
        grid=(b, s // TM),
        in_specs=[xs, cur, cur, cur, prev, cur, prev] + [_const_spec(c.shape) for c in consts],
        out_specs=xs,
        out_shape=jax.ShapeDtypeStruct(x.shape, F32),
        scratch_shapes=[pltpu.VMEM((HALO + TM, BRANCH_W), F32), pltpu.VMEM((SUBLANES - 1, HALO + TM, BRANCH_W), F32),
                        pltpu.VMEM((HALO + TM, BRANCH_W), F32)],
        compiler_params=_params(2),
    )(x, oa, od, glu, glu, cin, cin, *consts)


def _ffn_kernel(x_ref, xprev_ref, gpre_ref, gpost_ref, wu_ref, cw_ref, cb_ref, wd_ref, o_ref):
    x = x_ref[0]
    first = pl.program_id(1) == 0
    xe = jnp.concatenate([jnp.where(first, 0.0, xprev_ref[0]), x], axis=0)
    hb = _rms(xe, gpre_ref[...]).astype(BF16)

    def up(c):
        return tuple(_dot(hb, wu_ref[:, col:col + FFN_CHUNK]) for col in (c * FFN_CHUNK, D_FF + c * FFN_CHUNK))

    def conv(u, col):
        out = cb_ref[:, col:col + FFN_CHUNK]
        for k in range(FFN_CONV_WIDTH):
            lo = FFN_HALO - (FFN_CONV_WIDTH - 1) + k
            out = out + cw_ref[k:k + 1, col:col + FFN_CHUNK] * u[lo:lo + TM]
        return out

    def down(c, act):
        return _dot(act, wd_ref[c * FFN_CHUNK:(c + 1) * FFN_CHUNK, :])

    n_chunks = D_FF // FFN_CHUNK
    f = jnp.zeros((TM, D_MODEL), F32)
    nxt, acts = up(0), []
    for c in range(n_chunks):
        ug, uv = nxt
        if c + 1 < n_chunks:
            nxt = up(c + 1)
        if len(acts) == DOWN_GROUP:
            f += _dot(jnp.concatenate(acts, axis=1), wd_ref[(c - DOWN_GROUP) * FFN_CHUNK:c * FFN_CHUNK, :])
            acts = []
        acts.append((jax.nn.gelu(conv(ug, c * FFN_CHUNK), approximate=True)
                     * conv(uv, D_FF + c * FFN_CHUNK)).astype(BF16))
    f += _dot(jnp.concatenate(acts, axis=1), wd_ref[(n_chunks - len(acts)) * FFN_CHUNK:, :])
    o_ref[0] = x + _rms(f, gpost_ref[...])


def _ffn(x, gpre, gpost, wu, cw, cb, wd):
    b, s, d = x.shape
    xs = pl.BlockSpec((1, TM, d), lambda i, j: (i, j, 0))
    prev = pl.BlockSpec((1, FFN_HALO, d), lambda i, j: (i, jnp.maximum(j * (TM // FFN_HALO) - 1, 0), 0))
    return pl.pallas_call(
        _ffn_kernel,
        grid=(b, s // TM),
        in_specs=[xs, prev, _const_spec(gpre.shape), _const_spec(gpost.shape), _const_spec(wu.shape),
                  _const_spec(cw.shape), _const_spec(cb.shape), _const_spec(wd.shape)],
        out_specs=xs,
        out_shape=jax.ShapeDtypeStruct(x.shape, F32),
        compiler_params=_params(2),
    )(x, x, gpre, gpost, wu, cw, cb, wd)


def _rope_tables(seq):
    inv = 1.0 / (ROPE_THETA ** (jnp.arange(0, HEAD_DIM, 2, dtype=F32) / HEAD_DIM))
    ang = jnp.arange(seq, dtype=F32)[:, None] * inv[None, :]
    cos, sin = jnp.cos(ang), jnp.sin(ang)
    reps = LANES // HEAD_DIM
    return jnp.tile(jnp.concatenate([cos, cos], -1), (1, reps)), jnp.tile(jnp.concatenate([-sin, sin], -1), (1, reps))


def kernel(x, norm_mix_pre, norm_mix_post, norm_ffn_pre, norm_ffn_post, w_in, nsa_cmp_pos, nsa_cmp_w1, nsa_cmp_w2, swa_sinks, conv_w, conv_b, conv_ln_g, conv_ln_b, pool_w, pool_scale, w_branch, w_gate, w_o, ffn_w_up, ffn_conv_w, ffn_conv_b, ffn_w_down):
    bsz, seq, d = x.shape
    assert d == D_MODEL and seq % TM == 0 and TQ == TK and TM % TK == 0 and SWA_TQ == SWA_WINDOW
    assert CMP_LEN == 2 * CMP_STRIDE and SEL_LEN % CMP_STRIDE == 0
    assert seq % SWA_ROWS == 0 and SWA_ROWS % SWA_TQ == 0 and D_FF % FFN_CHUNK == 0
    assert seq // SEL_LEN <= HEAD_DIM and (seq // SEL_LEN) % SUBLANES == 0 and seq // CMP_STRIDE <= LANES
    depth = w_in.shape[0]
    cos, sin = _rope_tables(seq)
    perm, v_cols = _in_col_permutation()
    row = lambda v: v.reshape(1, -1)
    for l in range(depth):
        w_pad = jnp.concatenate([w_in[l], jnp.zeros((d, 1), w_in.dtype)], axis=1)
        w_perm = jnp.take(w_pad, jnp.asarray(np.where(perm < 0, w_in.shape[2], perm)), axis=1).astype(BF16)
        wvt = jnp.take(w_in[l], jnp.asarray(v_cols), axis=1).T.astype(BF16)
        q_all, q_rot, k_slc, k_win, k_d, cmp_raw, vt_a, vt_d, glu, cin, gates = _proj(
            x, row(norm_mix_pre[l]), w_perm, wvt, cos, sin)

        chunks = cmp_raw.reshape(bsz, 2 * N_KV, seq // CMP_STRIDE, CMP_STRIDE * HEAD_DIM)
        w2 = nsa_cmp_w2[l].astype(BF16)
        kc, kct = _compress(chunks, nsa_cmp_pos[l].reshape(2, 1, CMP_LEN * HEAD_DIM),
                            nsa_cmp_w1[l].astype(BF16), w2, jnp.swapaxes(w2, 1, 2))
        o_a = _nsa(q_all, q_rot, kc, kct, k_slc, k_win, vt_a, gates)
        sinks = jnp.broadcast_to(swa_sinks[l].reshape(N_KV, N_REP, 1), (N_KV, N_REP, SWA_TQ))
        o_d = _swa(q_all, k_d, vt_d, sinks)
        x = _mix(x, o_a, o_d, glu, cin, conv_w[l], row(conv_b[l]), row(conv_ln_g[l]), row(conv_ln_b[l]),
                 pool_w[l].astype(BF16), row(pool_scale[l]), row(norm_mix_pre[l]), row(norm_mix_post[l]),
                 w_gate[l].astype(BF16), w_branch[l].astype(BF16), w_o[l].astype(BF16))
        x = _ffn(x, row(norm_ffn_pre[l]), row(norm_ffn_post[l]), ffn_w_up[l].astype(BF16),
                 ffn_conv_w[l], row(ffn_conv_b[l]), ffn_w_down[l].astype(BF16))
    return x
```

```python
import numpy as np
import jax
import jax.numpy as jnp
from jax import lax
from jax.experimental import pallas as pl
from jax.experimental.pallas import tpu as pltpu

F32 = jnp.float32
BF16 = jnp.bfloat16

D_MODEL = 1024
HEAD_DIM = 64
HALF = HEAD_DIM // 2
ROPE_THETA = 10000.0
NORM_EPS = 1e-6
NEG = -1e30
M_INIT = -1e29
BRANCH_W = D_MODEL // 2
N_BRANCH = 4
N_HEADS = BRANCH_W // HEAD_DIM
N_KV = 2
N_REP = N_HEADS // N_KV
CMP_LEN = 32
CMP_STRIDE = 16
CMP_HIDDEN = 256
SEL_LEN = 64
SEL_SHIFT = 6
SEL_TOPK = 16
PER_SEL = SEL_LEN // CMP_STRIDE
RANK_CHAINS = 4
NSA_WINDOW = 512
FORCE_BONUS = 1e3
SWA_WINDOW = 128
CONV_WIDTH = 31
POOL_WINDOWS = (2, 4, 8, 16)
POOL_GROUP_CH = BRANCH_W // len(POOL_WINDOWS)
D_FF = ((8 * D_MODEL // 3) + 127) // 128 * 128
FFN_CONV_WIDTH = 3
IN_SIZES = (BRANCH_W, 3 * 2 * N_KV * HEAD_DIM, 3 * N_HEADS, BRANCH_W, 2 * N_KV * HEAD_DIM, 2 * BRANCH_W, BRANCH_W)

LANES = 128
SUBLANES = 8
TM = 256
TQ = 256
TK = 256
QL = N_REP * TQ
SWA_TQ = 128
SWA_ROWS = 512
ONES_ROWS = 16
LOG2E = 1.4426950408889634
HALO = 32
FFN_TM = 512
FFN_HALO = 8
FFN_CHUNK = 256
DOWN_GROUP = 3
VMEM_LIMIT = 56 * 1024 * 1024

ROPE_COLS = 2 * BRANCH_W + 3 * LANES
COL_CMP_K = ROPE_COLS
COL_CMP_V = COL_CMP_K + LANES
COL_GATE = COL_CMP_V + LANES
COL_B_IN = COL_GATE + N_KV * LANES
COL_C_IN = COL_B_IN + 2 * BRANCH_W
N_COLS = COL_C_IN + BRANCH_W
MXU_COLS = 256
N_VT = 3 * N_KV


def _in_col_permutation():
    off = np.cumsum((0,) + IN_SIZES)
    a_q, a_kv, a_gate, d_q, d_kv, b_in, c_in = (np.arange(off[i], off[i + 1]) for i in range(7))
    seg = lambda br, kv: a_kv[(br * 2 + kv) * LANES:(br * 2 + kv + 1) * LANES]
    gate = np.full((N_KV, LANES), -1, np.int64)
    for g in range(N_KV):
        for br in range(3):
            for r in range(N_REP):
                gate[g, br * N_REP + r] = a_gate[br * N_HEADS + g * N_REP + r]
    cols = np.concatenate([a_q, d_q, seg(1, 0), seg(2, 0), d_kv[:LANES],
                           seg(0, 0), seg(0, 1), gate.reshape(-1), b_in, c_in])
    assert cols.shape[0] == N_COLS
    return cols, np.concatenate([seg(1, 1), seg(2, 1), d_kv[LANES:]])


def _layer_spec(stacked, l):
    nd = stacked.ndim
    return pl.BlockSpec((None,) + stacked.shape[1:], lambda *_: (l,) + (0,) * (nd - 1), pipeline_mode=pl.Buffered(1))


def _params(n_grid):
    return pltpu.CompilerParams(dimension_semantics=("parallel",) * n_grid, vmem_limit_bytes=VMEM_LIMIT)


def _rms(x, g):
    return x * lax.rsqrt(jnp.mean(x * x, axis=-1, keepdims=True) + NORM_EPS) * g


def _dot(a, b):
    return jnp.dot(a, b, preferred_element_type=F32)


def _dot_t(a, b):
    return lax.dot_general(a, b, (((1,), (1,)), ((), ())), preferred_element_type=F32)


def _proj_kernel(x_ref, g_ref, w_ref, wvt_ref, cos_ref, sin_ref,
                 q_ref, qra_ref, ks_ref, kw_ref, kd_ref, cmp_ref, vta_ref, vtd_ref, glu_ref, cin_ref, gate_ref,
                 chunk_ref):
    hb = _rms(x_ref[0], g_ref[...]).astype(BF16)
    cos, sin = cos_ref[...], sin_ref[...]
    first_half = (lax.broadcasted_iota(jnp.int32, (TM, LANES), 1) & (HEAD_DIM - 1)) < HALF

    def mm(col, width=LANES):
        return _dot(hb, w_ref[:, col:col + width])

    def rope(z):
        partner = jnp.where(first_half, pltpu.roll(z, LANES - HALF, 1), pltpu.roll(z, HALF, 1))
        return z * cos + partner * sin

    def put_heads(ref, first, z):
        ref[0, first] = z[:, :HEAD_DIM].astype(ref.dtype)
        ref[0, first + 1] = z[:, HEAD_DIM:].astype(ref.dtype)

    lane = lax.broadcasted_iota(jnp.int32, (TM, LANES), 1)

    def put_wide(ref, first, z, fill=0.0):
        ref[0, first] = jnp.where(lane < HEAD_DIM, z, fill).astype(BF16)
        ref[0, first + 1] = jnp.where(lane < HEAD_DIM, pltpu.roll(z, HEAD_DIM, 1), fill).astype(BF16)

    def put_chunks(first, z):
        chunk_ref[...] = z
        for tok in range(CMP_STRIDE):
            rows = chunk_ref[pl.ds(tok, TM // CMP_STRIDE, stride=CMP_STRIDE), :]
            for g in range(N_KV):
                cmp_ref[0, first + g, :, tok * HEAD_DIM:(tok + 1) * HEAD_DIM] = rows[:, g * HEAD_DIM:(g + 1) * HEAD_DIM]

    scale = HEAD_DIM ** -0.5 * LOG2E

    def nsa_q(c):
        def put(z):
            put_heads(q_ref, 2 * c, z * scale)
            put_wide(qra_ref, 2 * c, rope(z * scale))
        return put

    def gate(g):
        def put(z):
            gate_ref[0, :, g * LANES:(g + 1) * LANES] = jax.nn.sigmoid(z)
        return put

    segments = [nsa_q(c) for c in range(BRANCH_W // LANES)]
    segments += [lambda z, c=c: put_heads(q_ref, N_HEADS + 2 * c, rope(z * scale))
                 for c in range(BRANCH_W // LANES)]

    def slc_k(z):
        blk = (pl.program_id(1) * TM + lax.broadcasted_iota(jnp.int32, (TM, LANES), 0)) >> SEL_SHIFT
        put_wide(ks_ref, 0, rope(z), jnp.where(lane - HEAD_DIM == blk, 1.0, 0.0))

    segments += [slc_k,
                 lambda z: put_wide(kw_ref, 0, rope(z)),
                 lambda z: put_heads(kd_ref, 0, rope(z)),
                 lambda z: put_chunks(0, z),
                 lambda z: put_chunks(N_KV, z)]
    segments += [gate(g) for g in range(N_KV)]
    per_dot = MXU_COLS // LANES
    for first in range(0, len(segments), per_dot):
        group = segments[first:first + per_dot]
        z = mm(first * LANES, len(group) * LANES)
        for i, put in enumerate(group):
            put(z[:, i * LANES:(i + 1) * LANES])
    vt = _dot_t(wvt_ref[...], hb).astype(BF16)
    for n in range(2 * N_KV):
        for c in range(TM // TK):
            vta_ref[0, n, c] = vt[n * HEAD_DIM:(n + 1) * HEAD_DIM, c * TK:(c + 1) * TK]
    for n in range(N_KV):
        rows = slice((2 * N_KV + n) * HEAD_DIM, (2 * N_KV + n + 1) * HEAD_DIM)
        for c in range(TM // SWA_TQ):
            vtd_ref[0, n, c] = vt[rows, c * SWA_TQ:(c + 1) * SWA_TQ]
    glu_ref[0] = mm(COL_B_IN, BRANCH_W) * jax.nn.sigmoid(mm(COL_B_IN + BRANCH_W, BRANCH_W))
    cin_ref[0] = mm(COL_C_IN, BRANCH_W)


def _proj(x, l, g, w, wvt, cos, sin):
    b, s, d = x.shape
    heads = lambda n, dt: (jax.ShapeDtypeStruct((b, n, s, HEAD_DIM), dt),
                           pl.BlockSpec((1, n, TM, HEAD_DIM), lambda i, j: (i, 0, j, 0)))
    rows = lambda n: (jax.ShapeDtypeStruct((b, s, n), F32), pl.BlockSpec((1, TM, n), lambda i, j: (i, j, 0)))
    vts = lambda n, t: (jax.ShapeDtypeStruct((b, n, s // t, HEAD_DIM, t), BF16),
                        pl.BlockSpec((1, n, TM // t, HEAD_DIM, t), lambda i, j: (i, 0, j, 0, 0)))
    wide = lambda n: (jax.ShapeDtypeStruct((b, n, s, LANES), BF16),
                      pl.BlockSpec((1, n, TM, LANES), lambda i, j: (i, 0, j, 0)))
    per_row = CMP_STRIDE * HEAD_DIM
    chunks = (jax.ShapeDtypeStruct((b, 2 * N_KV, s // CMP_STRIDE, per_row), F32),
              pl.BlockSpec((1, 2 * N_KV, TM // CMP_STRIDE, per_row), lambda i, j: (i, 0, j, 0)))
    outs = [heads(2 * N_HEADS, BF16), wide(N_HEADS), wide(N_KV), wide(N_KV), heads(N_KV, BF16), chunks,
            vts(2 * N_KV, TK), vts(N_KV, SWA_TQ), rows(BRANCH_W), rows(BRANCH_W), rows(N_KV * LANES)]
    return pl.pallas_call(
        _proj_kernel,
        grid=(b, s // TM),
        in_specs=[pl.BlockSpec((1, TM, d), lambda i, j: (i, j, 0)),
                  _layer_spec(g, l), _layer_spec(w, l), _layer_spec(wvt, l),
                  pl.BlockSpec((TM, LANES), lambda i, j: (j, 0)),
                  pl.BlockSpec((TM, LANES), lambda i, j: (j, 0))],
        out_specs=[o[1] for o in outs],
        out_shape=[o[0] for o in outs],
        scratch_shapes=[pltpu.VMEM((TM, LANES), F32)],
        compiler_params=_params(2),
    )(x, g, w, wvt, cos, sin)


def _compress_kernel(c_ref, pos_ref, w1_ref, w2_ref, w2t_ref, o_ref, ot_ref):
    c = c_ref[0, 0]
    n_chunk, half = c.shape
    pos = pos_ref[0]
    top = _dot((c + pos[:, :half]).astype(BF16), w1_ref[0, :half])
    bot = _dot((c + pos[:, half:]).astype(BF16), w1_ref[0, half:])
    hid = top + pltpu.roll(bot, n_chunk - 1, 0)
    act = jax.nn.gelu(hid, approximate=True).astype(BF16)
    out = _dot(act, w2_ref[0])
    row = lax.broadcasted_iota(jnp.int32, out.shape, 0)
    o_ref[0, 0] = jnp.where(row < n_chunk - 1, out, 0.0)
    out_t = _dot_t(w2t_ref[0], act)
    col = lax.broadcasted_iota(jnp.int32, out_t.shape, 1)
    ot_ref[0, 0] = jnp.where(col < n_chunk - 1, out_t, 0.0)


def _compress(chunks, l, pos, w1, w2, w2t):
    b, n, n_chunk, width = chunks.shape
    kv = lambda *dims: pl.BlockSpec((None, 1) + dims, lambda i, j: (l, j // N_KV, 0, 0))
    return pl.pallas_call(
        _compress_kernel,
        grid=(b, n),
        in_specs=[pl.BlockSpec((1, 1, n_chunk, width), lambda i, j: (i, j, 0, 0)),
                  kv(1, 2 * width), kv(2 * width, CMP_HIDDEN), kv(CMP_HIDDEN, HEAD_DIM), kv(HEAD_DIM, CMP_HIDDEN)],
        out_specs=[pl.BlockSpec((1, 1, n_chunk, HEAD_DIM), lambda i, j: (i, j, 0, 0)),
                   pl.BlockSpec((1, 1, HEAD_DIM, n_chunk), lambda i, j: (i, j, 0, 0))],
        out_shape=[jax.ShapeDtypeStruct((b, n, n_chunk, HEAD_DIM), F32),
                   jax.ShapeDtypeStruct((b, n, HEAD_DIM, n_chunk), F32)],
        compiler_params=_params(2),
    )(chunks, pos, w1, w2, w2t)


def _heads_on_lanes(a):
    return jnp.concatenate([a] * N_REP, axis=1)


def _with_ones(vt):
    return jnp.concatenate([vt, jnp.ones((ONES_ROWS, vt.shape[1]), vt.dtype)], axis=0)


def _scores(k_ref, q_refs, kt):
    k = k_ref[0, 0, pl.ds(pl.multiple_of(kt * TK, TK), TK), :]
    return tuple(_dot_t(k, q[...]) for q in q_refs)


def _absorb(k_ref, vt_ref, q_refs, tiles, state):
    ss = [_scores(k_ref, q_refs, kt) for kt, _ in tiles]
    ss = [s if bias is None else [x + bias for x in s] for s, (_, bias) in zip(ss, tiles)]
    ms = []
    for r, (m, _) in enumerate(state):
        for s in ss:
            m = jnp.maximum(m, jnp.max(s[r], axis=0, keepdims=True))
        ms.append(m)
    ps = [jnp.concatenate([jnp.exp2(s[r] - ms[r]).astype(BF16) for s in ss], axis=0) for r in range(N_REP)]
    vt1 = jnp.concatenate([_with_ones(vt_ref[0, 0, kt]) for kt, _ in tiles], axis=1)
    return tuple((m_new, jnp.exp2(m - m_new) * acc + _dot(vt1, p)) for p, m_new, (m, acc) in zip(ps, ms, state))


def _online_init():
    return tuple((jnp.full((1, TQ), M_INIT, F32), jnp.zeros((HEAD_DIM + ONES_ROWS, TQ), F32)) for _ in range(N_REP))


def _normed(state):
    return [acc[:HEAD_DIM] * (1.0 / acc[HEAD_DIM:HEAD_DIM + 1]) for _, acc in state]


def _band_attention(q_refs, k_ref, vt_ref, qi, window):
    tq = qi * TQ + lax.broadcasted_iota(jnp.int32, (TK, TQ), 1)
    row = lax.broadcasted_iota(jnp.int32, (TK, TQ), 0)
    tiles = []
    for d in range(window // TK + 1):
        key = (qi - d) * TK + row
        tiles.append((jnp.maximum(qi - d, 0), jnp.where((key <= tq) & (key > tq - window) & (key >= 0), 0.0, NEG)))
    return _absorb(k_ref, vt_ref, q_refs, tiles, _online_init())


def _heads_to_rows(o_t):
    tq = o_t.shape[1] // N_REP
    return jnp.concatenate([o_t[:, r * tq:(r + 1) * tq] for r in range(N_REP)], axis=0).T


def _nsa_kernel(q_ref, qr_ref, kc_ref, vct_ref, ks_ref, vst_ref, kw_ref, vwt_ref, gate_ref, o_ref,
                qaug_ref, psum_ref):
    qi = pl.program_id(2)
    q = q_ref[0].reshape(QL, HEAD_DIM)
    n_chunk = kc_ref.shape[2]
    n_slc = ks_ref.shape[2] // SEL_LEN

    win = _band_attention([qr_ref.at[0, r] for r in range(N_REP)], kw_ref, vwt_ref, qi, NSA_WINDOW)

    blk = lax.broadcasted_iota(jnp.int32, (n_chunk, TQ), 0)
    tqc = qi * TQ + lax.broadcasted_iota(jnp.int32, (n_chunk, TQ), 1)
    vis = (blk * CMP_STRIDE + CMP_LEN - 1 <= tqc) & (blk < n_chunk - 1)
    s = _dot_t(kc_ref[0, 0].astype(BF16), q) + _heads_on_lanes(jnp.where(vis, 0.0, NEG))
    e = jnp.exp2(s - jnp.maximum(jnp.max(s, axis=0, keepdims=True), M_INIT))
    p = e * (1.0 / jnp.maximum(jnp.sum(e, axis=0, keepdims=True), 1e-30))
    o_cmp = _dot(vct_ref[0, 0].astype(BF16), p.astype(BF16))

    p_sum = p[:, :TQ]
    for r in range(1, N_REP):
        p_sum = p_sum + p[:, r * TQ:(r + 1) * TQ]
    halves = range(TQ // LANES)
    for h in halves:
        psum_ref[h] = p_sum[:, h * LANES:(h + 1) * LANES]
    inside = [jnp.concatenate([psum_ref[h, pl.ds(k, n_slc, stride=PER_SEL), :] for h in halves], axis=1)
              for k in range(PER_SEL)]
    sb = lax.broadcasted_iota(jnp.int32, (n_slc, TQ), 0)
    imp = jnp.where(sb == 0, 0.0, pltpu.roll(inside[-1], 1, 0))
    for part in inside:
        imp = imp + part
    cur = (qi * TQ + lax.broadcasted_iota(jnp.int32, (n_slc, TQ), 1)) >> SEL_SHIFT
    forced = (sb == 0) | (sb == cur) | (sb == cur - 1)
    score = jnp.where(sb <= cur, imp + jnp.where(forced, FORCE_BONUS, 0.0), -1.0)
    ranks = [jnp.zeros((n_slc, TQ), F32) for _ in range(RANK_CHAINS)]
    for i in range(n_slc):
        si = score[i:i + 1, :]
        ranks[i % RANK_CHAINS] += jnp.where((si > score) | ((si == score) & (sb > i)), 1.0, 0.0)
    sel_bias = jnp.where(sum(ranks) < min(SEL_TOPK, n_slc), 0.0, NEG)

    zeros = lambda n: jnp.zeros((n, TQ), F32)
    bias_t = jnp.concatenate([zeros(HEAD_DIM), sel_bias, zeros(LANES - HEAD_DIM - n_slc)], axis=0).T.astype(BF16)
    for r in range(N_REP):
        qaug_ref[r] = qr_ref[0, r] + bias_t

    qaug = [qaug_ref.at[r] for r in range(N_REP)]

    def slc_pair(i, state):
        return _absorb(ks_ref, vst_ref, qaug, [(2 * i, None), (2 * i + 1, None)], state)

    slc = lax.fori_loop(0, qi // 2, slc_pair, _online_init())
    tq = lax.broadcasted_iota(jnp.int32, (TK, TQ), 1)
    row = lax.broadcasted_iota(jnp.int32, (TK, TQ), 0)
    diagonal = (qi, jnp.where(row <= tq, 0.0, NEG))
    slc = lax.cond(qi % 2 == 1,
                   lambda state: _absorb(ks_ref, vst_ref, qaug, [(qi - 1, None), diagonal], state),
                   lambda state: _absorb(ks_ref, vst_ref, qaug, [diagonal], state), slc)

    gate_t = gate_ref[0].T
    heads = []
    for r, (o_slc, o_win) in enumerate(zip(_normed(slc), _normed(win))):
        g = lambda br: gate_t[br * N_REP + r:br * N_REP + r + 1, :]
        heads.append(g(0) * o_cmp[:, r * TQ:(r + 1) * TQ] + g(1) * o_slc + g(2) * o_win)
    o_ref[0] = jnp.concatenate(heads, axis=0).T


def _nsa(q_all, q_rot, kc, kct, k_slc, k_win, vt, gates):
    b, _, s, _ = q_all.shape
    n_chunk = kc.shape[2]
    q_spec = lambda width: pl.BlockSpec((1, N_REP, TQ, width), lambda i, g, j: (i, g, j, 0))
    k_spec = pl.BlockSpec((1, 1, s, LANES), lambda i, g, j: (i, g, 0, 0))
    vt_spec = lambda first: pl.BlockSpec((1, 1, s // TK, HEAD_DIM, TK), lambda i, g, j: (i, first + g, 0, 0, 0))
    return pl.pallas_call(
        _nsa_kernel,
        grid=(b, N_KV, s // TQ),
        in_specs=[q_spec(HEAD_DIM), q_spec(LANES),
                  pl.BlockSpec((1, 1, n_chunk, HEAD_DIM), lambda i, g, j: (i, g, 0, 0)),
                  pl.BlockSpec((1, 1, HEAD_DIM, n_chunk), lambda i, g, j: (i, N_KV + g, 0, 0)),
                  k_spec, vt_spec(0), k_spec, vt_spec(N_KV),
                  pl.BlockSpec((1, TQ, LANES), lambda i, g, j: (i, j, g))],
        out_specs=pl.BlockSpec((1, TQ, N_REP * HEAD_DIM), lambda i, g, j: (i, j, g)),
        out_shape=jax.ShapeDtypeStruct((b, s, BRANCH_W), F32),
        scratch_shapes=[pltpu.VMEM((N_REP, TQ, LANES), BF16), pltpu.VMEM((TQ // LANES, n_chunk, LANES), F32)],
        compiler_params=_params(3),
    )(q_all, q_rot, kc, kct, k_slc, vt, k_win, vt, gates)


def _swa_kernel(q_ref, k_ref, vt_ref, sink_ref, o_ref):
    sink = jnp.concatenate([sink_ref[0, r:r + 1, :] for r in range(N_REP)], axis=1) * LOG2E
    row = lax.broadcasted_iota(jnp.int32, (2 * SWA_TQ, SWA_TQ), 0)
    col = lax.broadcasted_iota(jnp.int32, (2 * SWA_TQ, SWA_TQ), 1)
    subs = range(SWA_ROWS // SWA_TQ)
    qts = [pl.program_id(2) * len(subs) + sub for sub in subs]
    kts = [jnp.maximum(qt - 1, 0) for qt in qts]
    ss = []
    for sub, qt, kt in zip(subs, qts, kts):
        q = q_ref[0, :, sub * SWA_TQ:(sub + 1) * SWA_TQ, :].reshape(N_REP * SWA_TQ, HEAD_DIM)
        key, tq = kt * SWA_TQ + row, qt * SWA_TQ + col
        bias = _heads_on_lanes(jnp.where((key <= tq) & (key > tq - SWA_WINDOW), 0.0, NEG))
        ss.append(_dot_t(k_ref[0, 0, pl.ds(pl.multiple_of(kt * SWA_TQ, SWA_TQ), 2 * SWA_TQ), :], q) + bias)
    ms = [jnp.maximum(jnp.max(s, axis=0, keepdims=True), sink) for s in ss]
    ps = [jnp.exp2(s - m).astype(BF16) for s, m in zip(ss, ms)]
    accs = [_dot(_with_ones(jnp.concatenate([vt_ref[0, 0, kt], vt_ref[0, 0, kt + 1]], axis=1)), p)
            for kt, p in zip(kts, ps)]
    for sub, acc, m in zip(subs, accs, ms):
        out = acc[:HEAD_DIM] * (1.0 / (acc[HEAD_DIM:HEAD_DIM + 1] + jnp.exp2(sink - m)))
        o_ref[0, sub * SWA_TQ:(sub + 1) * SWA_TQ, :] = _heads_to_rows(out)


def _swa(q_all, k_d, vt, l, sinks):
    b, _, s, _ = q_all.shape
    return pl.pallas_call(
        _swa_kernel,
        grid=(b, N_KV, s // SWA_ROWS),
        in_specs=[pl.BlockSpec((1, N_REP, SWA_ROWS, HEAD_DIM), lambda i, g, j: (i, N_KV + g, j, 0)),
                  pl.BlockSpec((1, 1, s, HEAD_DIM), lambda i, g, j: (i, g, 0, 0)),
                  pl.BlockSpec((1, 1, s // SWA_TQ, HEAD_DIM, SWA_TQ), lambda i, g, j: (i, g, 0, 0, 0)),
                  pl.BlockSpec((None, 1, N_REP, SWA_TQ), lambda i, g, j: (l, g, 0, 0))],
        out_specs=pl.BlockSpec((1, SWA_ROWS, N_REP * HEAD_DIM), lambda i, g, j: (i, j, g)),
        out_shape=jax.ShapeDtypeStruct((b, s, BRANCH_W), F32),
        compiler_params=_params(3),
    )(q_all, k_d, vt, sinks)


def _mix_kernel(x_ref, oa_ref, od_ref, glu_ref, glu_prev_ref, cin_ref, cin_prev_ref,
                cw_ref, cb_ref, lg_ref, lb_ref, pw_ref, ps_ref, gpre_ref, gpost_ref, wg_ref, wb_ref, wo_ref,
                o_ref, gext, gshift, cext):
    x = x_ref[0]
    hb = _rms(x, gpre_ref[...]).astype(BF16)

    def gated_up(n, branch):
        gate = jax.nn.sigmoid(_dot(hb, wg_ref[:, n * D_MODEL:(n + 1) * D_MODEL]))
        return gate * _dot(branch.astype(BF16), wb_ref[n])

    first = pl.program_id(1) == 0
    gext[:HALO] = jnp.where(first, 0.0, glu_prev_ref[0])
    gext[HALO:] = glu_ref[0]
    cext[:HALO] = jnp.where(first, 0.0, cin_prev_ref[0])
    cext[HALO:] = cin_ref[0]
    n_shift = HALO + TM - SUBLANES
    for ph in range(1, SUBLANES):
        gshift[ph - 1, :n_shift] = gext[pl.ds(ph, n_shift), :]

    mix = gated_up(0, oa_ref[0])

    acc = jnp.zeros((TM, BRANCH_W), F32) + cb_ref[...]
    for k in range(CONV_WIDTH):
        base, ph = divmod(HALO - (CONV_WIDTH - 1) + k, SUBLANES)
        rows = pl.ds(base * SUBLANES, TM)
        acc += cw_ref[k:k + 1, :] * (gext[rows, :] if ph == 0 else gshift[ph - 1, rows, :])
    mix += gated_up(N_BRANCH - 1, od_ref[0])
    mu = jnp.mean(acc, axis=-1, keepdims=True)
    cen = acc - mu
    y = cen * lax.rsqrt(jnp.mean(cen * cen, axis=-1, keepdims=True) + NORM_EPS) * lg_ref[...] + lb_ref[...]
    mix += gated_up(1, y * jax.nn.sigmoid(y))

    t = pl.program_id(1) * TM + lax.broadcasted_iota(jnp.int32, (TM, 1), 0)
    pooled = []
    for g, win in enumerate(POOL_WINDOWS):
        lanes = pl.ds(g * POOL_GROUP_CH, POOL_GROUP_CH)
        tot = cext[pl.ds(HALO, TM), lanes]
        for d in range(1, win):
            tot += cext[pl.ds(HALO - d, TM), lanes]
        mean_minus_token = tot / jnp.minimum(t + 1, win).astype(F32) - cext[pl.ds(HALO, TM), lanes]
        pooled.append(_dot(mean_minus_token.astype(BF16), pw_ref[g]))
    mix += gated_up(2, jnp.concatenate(pooled, axis=1) * ps_ref[...])

    o_ref[0] = x + _rms(_dot(mix.astype(BF16), wo_ref[...]), gpost_ref[...])


def _mix(x, oa, od, glu, cin, l, cw, cb, lg, lb, pw, ps, gpre, gpost, wg, wb, wo):
    b, s, d = x.shape
    xs = pl.BlockSpec((1, TM, d), lambda i, j: (i, j, 0))
    cur = pl.BlockSpec((1, TM, BRANCH_W), lambda i, j: (i, j, 0))
    prev = pl.BlockSpec((1, HALO, BRANCH_W), lambda i, j: (i, jnp.maximum(j * (TM // HALO) - 1, 0), 0))
    consts = (cw, cb, lg, lb, pw, ps, gpre, gpost, wg, wb, wo)
    return pl.pallas_call(
        _mix_kernel,
        grid=(b, s // TM),
        in_specs=[xs, cur, cur, cur, prev, cur, prev] + [_layer_spec(c, l) for c in consts],
        out_specs=xs,
        out_shape=jax.ShapeDtypeStruct(x.shape, F32),
        scratch_shapes=[pltpu.VMEM((HALO + TM, BRANCH_W), F32), pltpu.VMEM((SUBLANES - 1, HALO + TM, BRANCH_W), F32),
                        pltpu.VMEM((HALO + TM, BRANCH_W), F32)],
        compiler_params=_params(2),
    )(x, oa, od, glu, glu, cin, cin, *consts)


def _ffn_kernel(x_ref, xprev_ref, gpre_ref, gpost_ref, wu_ref, cw_ref, cb_ref, wd_ref, o_ref):
    x = x_ref[0]
    first = pl.program_id(1) == 0
    xe = jnp.concatenate([jnp.where(first, 0.0, xprev_ref[0]), x], axis=0)
    hb = _rms(xe, gpre_ref[...]).astype(BF16)

    def up(c):
        return tuple(_dot(hb, wu_ref[:, col:col + FFN_CHUNK]) for col in (c * FFN_CHUNK, D_FF + c * FFN_CHUNK))

    def conv(u, col):
        out = cb_ref[:, col:col + FFN_CHUNK]
        for k in range(FFN_CONV_WIDTH):
            lo = FFN_HALO - (FFN_CONV_WIDTH - 1) + k
            out = out + cw_ref[k:k + 1, col:col + FFN_CHUNK] * u[lo:lo + FFN_TM]
        return out

    def down(c, act):
        return _dot(act, wd_ref[c * FFN_CHUNK:(c + 1) * FFN_CHUNK, :])

    n_chunks = D_FF // FFN_CHUNK
    f = jnp.zeros((FFN_TM, D_MODEL), F32)
    nxt, acts = up(0), []
    for c in range(n_chunks):
        ug, uv = nxt
        if c + 1 < n_chunks:
            nxt = up(c + 1)
        if len(acts) == DOWN_GROUP:
            f += _dot(jnp.concatenate(acts, axis=1), wd_ref[(c - DOWN_GROUP) * FFN_CHUNK:c * FFN_CHUNK, :])
            acts = []
        acts.append((jax.nn.gelu(conv(ug, c * FFN_CHUNK), approximate=True)
                     * conv(uv, D_FF + c * FFN_CHUNK)).astype(BF16))
    f += _dot(jnp.concatenate(acts, axis=1), wd_ref[(n_chunks - len(acts)) * FFN_CHUNK:, :])
    o_ref[0] = x + _rms(f, gpost_ref[...])


def _ffn(x, l, gpre, gpost, wu, cw, cb, wd):
    b, s, d = x.shape
    xs = pl.BlockSpec((1, FFN_TM, d), lambda i, j: (i, j, 0))
    prev = pl.BlockSpec((1, FFN_HALO, d), lambda i, j: (i, jnp.maximum(j * (FFN_TM // FFN_HALO) - 1, 0), 0))
    return pl.pallas_call(
        _ffn_kernel,
        grid=(b, s // FFN_TM),
        in_specs=[xs, prev] + [_layer_spec(c, l) for c in (gpre, gpost, wu, cw, cb, wd)],
        out_specs=xs,
        out_shape=jax.ShapeDtypeStruct(x.shape, F32),
        compiler_params=_params(2),
    )(x, x, gpre, gpost, wu, cw, cb, wd)


def _rope_tables(seq):
    inv = 1.0 / (ROPE_THETA ** (jnp.arange(0, HEAD_DIM, 2, dtype=F32) / HEAD_DIM))
    ang = jnp.arange(seq, dtype=F32)[:, None] * inv[None, :]
    cos, sin = jnp.cos(ang), jnp.sin(ang)
    reps = LANES // HEAD_DIM
    return jnp.tile(jnp.concatenate([cos, cos], -1), (1, reps)), jnp.tile(jnp.concatenate([-sin, sin], -1), (1, reps))


def kernel(x, norm_mix_pre, norm_mix_post, norm_ffn_pre, norm_ffn_post, w_in, nsa_cmp_pos, nsa_cmp_w1, nsa_cmp_w2, swa_sinks, conv_w, conv_b, conv_ln_g, conv_ln_b, pool_w, pool_scale, w_branch, w_gate, w_o, ffn_w_up, ffn_conv_w, ffn_conv_b, ffn_w_down):
    bsz, seq, d = x.shape
    assert d == D_MODEL and seq % TM == 0 and TQ == TK and TM % TK == 0 and SWA_TQ == SWA_WINDOW
    assert CMP_LEN == 2 * CMP_STRIDE and SEL_LEN % CMP_STRIDE == 0
    assert seq % FFN_TM == 0 and seq % SWA_ROWS == 0 and SWA_ROWS % SWA_TQ == 0 and D_FF % FFN_CHUNK == 0
    assert seq // SEL_LEN <= HEAD_DIM and (seq // SEL_LEN) % SUBLANES == 0 and seq // CMP_STRIDE <= LANES
    depth = w_in.shape[0]
    cos, sin = _rope_tables(seq)
    perm, v_cols = _in_col_permutation()
    rows = lambda v: v.reshape(depth, 1, -1)
    w_pad = jnp.concatenate([w_in, jnp.zeros((depth, d, 1), w_in.dtype)], axis=2)
    w_perm = jnp.take(w_pad, jnp.asarray(np.where(perm < 0, w_in.shape[2], perm)), axis=2).astype(BF16)
    wvt = jnp.swapaxes(jnp.take(w_in, jnp.asarray(v_cols), axis=2), 1, 2).astype(BF16)
    cmp_pos = nsa_cmp_pos.reshape(depth, 2, 1, CMP_LEN * HEAD_DIM)
    cmp_w1, cmp_w2 = nsa_cmp_w1.astype(BF16), nsa_cmp_w2.astype(BF16)
    sinks = jnp.broadcast_to(swa_sinks.reshape(depth, N_KV, N_REP, 1), (depth, N_KV, N_REP, SWA_TQ))
    mix_params = (conv_w, rows(conv_b), rows(conv_ln_g), rows(conv_ln_b), pool_w.astype(BF16), rows(pool_scale),
                  rows(norm_mix_pre), rows(norm_mix_post), w_gate.astype(BF16), w_branch.astype(BF16), w_o.astype(BF16))
    ffn_params = (rows(norm_ffn_pre), rows(norm_ffn_post), ffn_w_up.astype(BF16), ffn_conv_w, rows(ffn_conv_b),
                  ffn_w_down.astype(BF16))
    for l in range(depth):
        q_all, q_rot, k_slc, k_win, k_d, chunks, vt_a, vt_d, glu, cin, gates = _proj(
            x, l, rows(norm_mix_pre), w_perm, wvt, cos, sin)
        kc, kct = _compress(chunks, l, cmp_pos, cmp_w1, cmp_w2, jnp.swapaxes(cmp_w2, 2, 3))
        o_a = _nsa(q_all, q_rot, kc, kct, k_slc, k_win, vt_a, gates)
        o_d = _swa(q_all, k_d, vt_d, l, sinks)
        x = _mix(x, o_a, o_d, glu, cin, l, *mix_params)
        x = _ffn(x, l, *ffn_params)
    return x
```

```python
import numpy as np
import jax
import jax.numpy as jnp
from jax import lax
from jax.experimental import pallas as pl
from jax.experimental.pallas import tpu as pltpu

F32 = jnp.float32
BF16 = jnp.bfloat16

D_MODEL = 1024
HEAD_DIM = 64
HALF = HEAD_DIM // 2
ROPE_THETA = 10000.0
NORM_EPS = 1e-6
NEG = -1e30
M_INIT = -1e29
BRANCH_W = D_MODEL // 2
N_BRANCH = 4
N_HEADS = BRANCH_W // HEAD_DIM
N_KV = 2
N_REP = N_HEADS // N_KV
CMP_LEN = 32
CMP_STRIDE = 16
CMP_HIDDEN = 256
SEL_LEN = 64
SEL_SHIFT = 6
SEL_TOPK = 16
PER_SEL = SEL_LEN // CMP_STRIDE
RANK_CHAINS = 4
NSA_WINDOW = 512
FORCE_BONUS = 1e3
SWA_WINDOW = 128
CONV_WIDTH = 31
POOL_WINDOWS = (2, 4, 8, 16)
POOL_GROUP_CH = BRANCH_W // len(POOL_WINDOWS)
D_FF = ((8 * D_MODEL // 3) + 127) // 128 * 128
FFN_CONV_WIDTH = 3
IN_SIZES = (BRANCH_W, 3 * 2 * N_KV * HEAD_DIM, 3 * N_HEADS, BRANCH_W, 2 * N_KV * HEAD_DIM, 2 * BRANCH_W, BRANCH_W)

LANES = 128
SUBLANES = 8
TM = 512
TQ = 256
TK = 256
QL = N_REP * TQ
SWA_TQ = 128
SWA_ROWS = 512
ONES_ROWS = 16
LOG2E = 1.4426950408889634
HALO = 32
FFN_TM = 512
MIX_TM = 512
FFN_HALO = 8
FFN_CHUNK = 256
DOWN_GROUP = 3
VMEM_LIMIT = 56 * 1024 * 1024

ROPE_COLS = 2 * BRANCH_W + 3 * LANES
COL_CMP_K = ROPE_COLS
COL_CMP_V = COL_CMP_K + LANES
COL_GATE = COL_CMP_V + LANES
COL_B_IN = COL_GATE + N_KV * LANES
COL_C_IN = COL_B_IN + 2 * BRANCH_W
N_COLS = COL_C_IN + BRANCH_W
MXU_COLS = 256
N_VT = 3 * N_KV


def _in_col_permutation():
    off = np.cumsum((0,) + IN_SIZES)
    a_q, a_kv, a_gate, d_q, d_kv, b_in, c_in = (np.arange(off[i], off[i + 1]) for i in range(7))
    seg = lambda br, kv: a_kv[(br * 2 + kv) * LANES:(br * 2 + kv + 1) * LANES]
    gate = np.full((N_KV, LANES), -1, np.int64)
    for g in range(N_KV):
        for br in range(3):
            for r in range(N_REP):
                gate[g, br * N_REP + r] = a_gate[br * N_HEADS + g * N_REP + r]
    cols = np.concatenate([a_q, d_q, seg(1, 0), seg(2, 0), d_kv[:LANES],
                           seg(0, 0), seg(0, 1), gate.reshape(-1), b_in, c_in])
    assert cols.shape[0] == N_COLS
    return cols, np.concatenate([seg(1, 1), seg(2, 1), d_kv[LANES:]])


def _take_cols(w, cols):
    runs = []
    for c in (int(c) for c in cols):
        if runs and ((c < 0 and runs[-1][0] < 0) or (c >= 0 and runs[-1][0] >= 0 and c == sum(runs[-1]))):
            runs[-1][1] += 1
        else:
            runs.append([c, 1])
    parts = [jnp.zeros(w.shape[:-1] + (n,), w.dtype) if c < 0 else w[..., c:c + n] for c, n in runs]
    return jnp.concatenate(parts, axis=-1)


def _layer_spec(stacked, l):
    nd = stacked.ndim
    return pl.BlockSpec((None,) + stacked.shape[1:], lambda *_: (l,) + (0,) * (nd - 1), pipeline_mode=pl.Buffered(1))


def _params(n_grid):
    return pltpu.CompilerParams(dimension_semantics=("parallel",) * n_grid, vmem_limit_bytes=VMEM_LIMIT)


def _rms(x, g):
    return x * lax.rsqrt(jnp.mean(x * x, axis=-1, keepdims=True) + NORM_EPS) * g


def _dot(a, b):
    return jnp.dot(a, b, preferred_element_type=F32)


def _dot_t(a, b):
    return lax.dot_general(a, b, (((1,), (1,)), ((), ())), preferred_element_type=F32)


def _proj_kernel(x_ref, g_ref, w_ref, wvt_ref, cos_ref, sin_ref,
                 q_ref, qra_ref, ks_ref, kw_ref, kd_ref, cmp_ref, vta_ref, vtd_ref, glu_ref, cin_ref, gate_ref,
                 chunk_ref):
    hb = _rms(x_ref[0], g_ref[...]).astype(BF16)
    cos, sin = cos_ref[...], sin_ref[...]
    first_half = (lax.broadcasted_iota(jnp.int32, (TM, LANES), 1) & (HEAD_DIM - 1)) < HALF

    def mm(col, width=LANES):
        return _dot(hb, w_ref[:, col:col + width])

    def rope(z):
        partner = jnp.where(first_half, pltpu.roll(z, LANES - HALF, 1), pltpu.roll(z, HALF, 1))
        return z * cos + partner * sin

    def put_heads(ref, first, z):
        ref[0, first] = z[:, :HEAD_DIM].astype(ref.dtype)
        ref[0, first + 1] = z[:, HEAD_DIM:].astype(ref.dtype)

    lane = lax.broadcasted_iota(jnp.int32, (TM, LANES), 1)

    def put_wide(ref, first, z, fill=0.0):
        ref[0, first] = jnp.where(lane < HEAD_DIM, z, fill).astype(BF16)
        ref[0, first + 1] = jnp.where(lane < HEAD_DIM, pltpu.roll(z, HEAD_DIM, 1), fill).astype(BF16)

    def put_chunks(first, z):
        chunk_ref[...] = z
        for tok in range(CMP_STRIDE):
            rows = chunk_ref[pl.ds(tok, TM // CMP_STRIDE, stride=CMP_STRIDE), :]
            for g in range(N_KV):
                cmp_ref[0, first + g, :, tok * HEAD_DIM:(tok + 1) * HEAD_DIM] = rows[:, g * HEAD_DIM:(g + 1) * HEAD_DIM]

    scale = HEAD_DIM ** -0.5 * LOG2E

    def nsa_q(c):
        def put(z):
            put_heads(q_ref, 2 * c, z * scale)
            put_wide(qra_ref, 2 * c, rope(z * scale))
        return put

    def gate(g):
        def put(z):
            gate_ref[0, :, g * LANES:(g + 1) * LANES] = jax.nn.sigmoid(z)
        return put

    segments = [nsa_q(c) for c in range(BRANCH_W // LANES)]
    segments += [lambda z, c=c: put_heads(q_ref, N_HEADS + 2 * c, rope(z * scale))
                 for c in range(BRANCH_W // LANES)]

    def slc_k(z):
        blk = (pl.program_id(1) * TM + lax.broadcasted_iota(jnp.int32, (TM, LANES), 0)) >> SEL_SHIFT
        put_wide(ks_ref, 0, rope(z), jnp.where(lane - HEAD_DIM == blk, 1.0, 0.0))

    segments += [slc_k,
                 lambda z: put_wide(kw_ref, 0, rope(z)),
                 lambda z: put_heads(kd_ref, 0, rope(z)),
                 lambda z: put_chunks(0, z),
                 lambda z: put_chunks(N_KV, z)]
    segments += [gate(g) for g in range(N_KV)]
    per_dot = MXU_COLS // LANES
    for first in range(0, len(segments), per_dot):
        group = segments[first:first + per_dot]
        z = mm(first * LANES, len(group) * LANES)
        for i, put in enumerate(group):
            put(z[:, i * LANES:(i + 1) * LANES])
    vt = _dot_t(wvt_ref[...], hb).astype(BF16)
    for n in range(2 * N_KV):
        for c in range(TM // TK):
            vta_ref[0, n, c] = vt[n * HEAD_DIM:(n + 1) * HEAD_DIM, c * TK:(c + 1) * TK]
    for n in range(N_KV):
        rows = slice((2 * N_KV + n) * HEAD_DIM, (2 * N_KV + n + 1) * HEAD_DIM)
        for c in range(TM // SWA_TQ):
            vtd_ref[0, n, c] = vt[rows, c * SWA_TQ:(c + 1) * SWA_TQ]
    glu_ref[0] = mm(COL_B_IN, BRANCH_W) * jax.nn.sigmoid(mm(COL_B_IN + BRANCH_W, BRANCH_W))
    cin_ref[0] = mm(COL_C_IN, BRANCH_W)


def _proj(x, l, g, w, wvt, cos, sin):
    b, s, d = x.shape
    heads = lambda n, dt: (jax.ShapeDtypeStruct((b, n, s, HEAD_DIM), dt),
                           pl.BlockSpec((1, n, TM, HEAD_DIM), lambda i, j: (i, 0, j, 0)))
    rows = lambda n: (jax.ShapeDtypeStruct((b, s, n), F32), pl.BlockSpec((1, TM, n), lambda i, j: (i, j, 0)))
    vts = lambda n, t: (jax.ShapeDtypeStruct((b, n, s // t, HEAD_DIM, t), BF16),
                        pl.BlockSpec((1, n, TM // t, HEAD_DIM, t), lambda i, j: (i, 0, j, 0, 0)))
    wide = lambda n: (jax.ShapeDtypeStruct((b, n, s, LANES), BF16),
                      pl.BlockSpec((1, n, TM, LANES), lambda i, j: (i, 0, j, 0)))
    per_row = CMP_STRIDE * HEAD_DIM
    chunks = (jax.ShapeDtypeStruct((b, 2 * N_KV, s // CMP_STRIDE, per_row), F32),
              pl.BlockSpec((1, 2 * N_KV, TM // CMP_STRIDE, per_row), lambda i, j: (i, 0, j, 0)))
    outs = [heads(2 * N_HEADS, BF16), wide(N_HEADS), wide(N_KV), wide(N_KV), heads(N_KV, BF16), chunks,
            vts(2 * N_KV, TK), vts(N_KV, SWA_TQ), rows(BRANCH_W), rows(BRANCH_W), rows(N_KV * LANES)]
    return pl.pallas_call(
        _proj_kernel,
        grid=(b, s // TM),
        in_specs=[pl.BlockSpec((1, TM, d), lambda i, j: (i, j, 0)),
                  _layer_spec(g, l), _layer_spec(w, l), _layer_spec(wvt, l),
                  pl.BlockSpec((TM, LANES), lambda i, j: (j, 0)),
                  pl.BlockSpec((TM, LANES), lambda i, j: (j, 0))],
        out_specs=[o[1] for o in outs],
        out_shape=[o[0] for o in outs],
        scratch_shapes=[pltpu.VMEM((TM, LANES), F32)],
        compiler_params=_params(2),
    )(x, g, w, wvt, cos, sin)


def _compress_kernel(c_ref, pos_ref, w1_ref, w2_ref, w2t_ref, o_ref, ot_ref):
    c = c_ref[0, 0]
    n_chunk, half = c.shape
    pos = pos_ref[0]
    top = _dot((c + pos[:, :half]).astype(BF16), w1_ref[0, :half])
    bot = _dot((c + pos[:, half:]).astype(BF16), w1_ref[0, half:])
    hid = top + pltpu.roll(bot, n_chunk - 1, 0)
    act = jax.nn.gelu(hid, approximate=True).astype(BF16)
    out = _dot(act, w2_ref[0])
    row = lax.broadcasted_iota(jnp.int32, out.shape, 0)
    o_ref[0, 0] = jnp.where(row < n_chunk - 1, out, 0.0)
    out_t = _dot_t(w2t_ref[0], act)
    col = lax.broadcasted_iota(jnp.int32, out_t.shape, 1)
    ot_ref[0, 0] = jnp.where(col < n_chunk - 1, out_t, 0.0)


def _compress(chunks, l, pos, w1, w2, w2t):
    b, n, n_chunk, width = chunks.shape
    kv = lambda *dims: pl.BlockSpec((None, 1) + dims, lambda i, j: (l, j // N_KV, 0, 0))
    return pl.pallas_call(
        _compress_kernel,
        grid=(b, n),
        in_specs=[pl.BlockSpec((1, 1, n_chunk, width), lambda i, j: (i, j, 0, 0)),
                  kv(1, 2 * width), kv(2 * width, CMP_HIDDEN), kv(CMP_HIDDEN, HEAD_DIM), kv(HEAD_DIM, CMP_HIDDEN)],
        out_specs=[pl.BlockSpec((1, 1, n_chunk, HEAD_DIM), lambda i, j: (i, j, 0, 0)),
                   pl.BlockSpec((1, 1, HEAD_DIM, n_chunk), lambda i, j: (i, j, 0, 0))],
        out_shape=[jax.ShapeDtypeStruct((b, n, n_chunk, HEAD_DIM), F32),
                   jax.ShapeDtypeStruct((b, n, HEAD_DIM, n_chunk), F32)],
        compiler_params=_params(2),
    )(chunks, pos, w1, w2, w2t)


def _heads_on_lanes(a):
    return jnp.concatenate([a] * N_REP, axis=1)


def _with_ones(vt):
    return jnp.concatenate([vt, jnp.ones((ONES_ROWS, vt.shape[1]), vt.dtype)], axis=0)


def _scores(k_ref, q_refs, kt):
    k = k_ref[0, 0, pl.ds(pl.multiple_of(kt * TK, TK), TK), :]
    return tuple(_dot_t(k, q[...]) for q in q_refs)


def _absorb(k_ref, vt_ref, q_refs, tiles, state):
    ss = [_scores(k_ref, q_refs, kt) for kt, _ in tiles]
    ss = [s if bias is None else [x + bias for x in s] for s, (_, bias) in zip(ss, tiles)]
    ms = []
    for r, (m, _) in enumerate(state):
        for s in ss:
            m = jnp.maximum(m, jnp.max(s[r], axis=0, keepdims=True))
        ms.append(m)
    ps = [jnp.concatenate([jnp.exp2(s[r] - ms[r]).astype(BF16) for s in ss], axis=0) for r in range(N_REP)]
    vt1 = jnp.concatenate([_with_ones(vt_ref[0, 0, kt]) for kt, _ in tiles], axis=1)
    return tuple((m_new, jnp.exp2(m - m_new) * acc + _dot(vt1, p)) for p, m_new, (m, acc) in zip(ps, ms, state))


def _online_init():
    return tuple((jnp.full((1, TQ), M_INIT, F32), jnp.zeros((HEAD_DIM + ONES_ROWS, TQ), F32)) for _ in range(N_REP))


def _normed(state):
    return [acc[:HEAD_DIM] * (1.0 / acc[HEAD_DIM:HEAD_DIM + 1]) for _, acc in state]


def _band_attention(q_refs, k_ref, vt_ref, qi, window):
    tq = qi * TQ + lax.broadcasted_iota(jnp.int32, (TK, TQ), 1)
    row = lax.broadcasted_iota(jnp.int32, (TK, TQ), 0)
    tiles = []
    for d in range(window // TK + 1):
        key = (qi - d) * TK + row
        tiles.append((jnp.maximum(qi - d, 0), jnp.where((key <= tq) & (key > tq - window) & (key >= 0), 0.0, NEG)))
    return _absorb(k_ref, vt_ref, q_refs, tiles, _online_init())


def _heads_to_rows(o_t):
    tq = o_t.shape[1] // N_REP
    return jnp.concatenate([o_t[:, r * tq:(r + 1) * tq] for r in range(N_REP)], axis=0).T


def _nsa_kernel(q_ref, qr_ref, kc_ref, vct_ref, ks_ref, vst_ref, kw_ref, vwt_ref, gate_ref, o_ref,
                qaug_ref, psum_ref):
    qi = pl.program_id(2)
    q = q_ref[0].reshape(QL, HEAD_DIM)
    n_chunk = kc_ref.shape[2]
    n_slc = ks_ref.shape[2] // SEL_LEN

    win = _band_attention([qr_ref.at[0, r] for r in range(N_REP)], kw_ref, vwt_ref, qi, NSA_WINDOW)

    blk = lax.broadcasted_iota(jnp.int32, (n_chunk, TQ), 0)
    tqc = qi * TQ + lax.broadcasted_iota(jnp.int32, (n_chunk, TQ), 1)
    vis = (blk * CMP_STRIDE + CMP_LEN - 1 <= tqc) & (blk < n_chunk - 1)
    s = _dot_t(kc_ref[0, 0].astype(BF16), q) + _heads_on_lanes(jnp.where(vis, 0.0, NEG))
    e = jnp.exp2(s - jnp.maximum(jnp.max(s, axis=0, keepdims=True), M_INIT))
    p = e * (1.0 / jnp.maximum(jnp.sum(e, axis=0, keepdims=True), 1e-30))
    o_cmp = _dot(vct_ref[0, 0].astype(BF16), p.astype(BF16))

    p_sum = p[:, :TQ]
    for r in range(1, N_REP):
        p_sum = p_sum + p[:, r * TQ:(r + 1) * TQ]
    halves = range(TQ // LANES)
    for h in halves:
        psum_ref[h] = p_sum[:, h * LANES:(h + 1) * LANES]
    inside = [jnp.concatenate([psum_ref[h, pl.ds(k, n_slc, stride=PER_SEL), :] for h in halves], axis=1)
              for k in range(PER_SEL)]
    sb = lax.broadcasted_iota(jnp.int32, (n_slc, TQ), 0)
    imp = jnp.where(sb == 0, 0.0, pltpu.roll(inside[-1], 1, 0))
    for part in inside:
        imp = imp + part
    cur = (qi * TQ + lax.broadcasted_iota(jnp.int32, (n_slc, TQ), 1)) >> SEL_SHIFT
    forced = (sb == 0) | (sb == cur) | (sb == cur - 1)
    score = jnp.where(sb <= cur, imp + jnp.where(forced, FORCE_BONUS, 0.0), -1.0)
    ranks = [jnp.zeros((n_slc, TQ), F32) for _ in range(RANK_CHAINS)]
    for i in range(n_slc):
        si = score[i:i + 1, :]
        ranks[i % RANK_CHAINS] += jnp.where((si > score) | ((si == score) & (sb > i)), 1.0, 0.0)
    sel_bias = jnp.where(sum(ranks) < min(SEL_TOPK, n_slc), 0.0, NEG)

    zeros = lambda n: jnp.zeros((n, TQ), F32)
    bias_t = jnp.concatenate([zeros(HEAD_DIM), sel_bias, zeros(LANES - HEAD_DIM - n_slc)], axis=0).T.astype(BF16)
    for r in range(N_REP):
        qaug_ref[r] = qr_ref[0, r] + bias_t

    qaug = [qaug_ref.at[r] for r in range(N_REP)]

    def slc_pair(i, state):
        return _absorb(ks_ref, vst_ref, qaug, [(2 * i, None), (2 * i + 1, None)], state)

    slc = lax.fori_loop(0, qi // 2, slc_pair, _online_init())
    tq = lax.broadcasted_iota(jnp.int32, (TK, TQ), 1)
    row = lax.broadcasted_iota(jnp.int32, (TK, TQ), 0)
    diagonal = (qi, jnp.where(row <= tq, 0.0, NEG))
    slc = lax.cond(qi % 2 == 1,
                   lambda state: _absorb(ks_ref, vst_ref, qaug, [(qi - 1, None), diagonal], state),
                   lambda state: _absorb(ks_ref, vst_ref, qaug, [diagonal], state), slc)

    gate_t = gate_ref[0].T
    heads = []
    for r, (o_slc, o_win) in enumerate(zip(_normed(slc), _normed(win))):
        g = lambda br: gate_t[br * N_REP + r:br * N_REP + r + 1, :]
        heads.append(g(0) * o_cmp[:, r * TQ:(r + 1) * TQ] + g(1) * o_slc + g(2) * o_win)
    o_ref[0] = jnp.concatenate(heads, axis=0).T


def _nsa(q_all, q_rot, kc, kct, k_slc, k_win, vt, gates):
    b, _, s, _ = q_all.shape
    n_chunk = kc.shape[2]
    q_spec = lambda width: pl.BlockSpec((1, N_REP, TQ, width), lambda i, g, j: (i, g, j, 0))
    k_spec = pl.BlockSpec((1, 1, s, LANES), lambda i, g, j: (i, g, 0, 0))
    vt_spec = lambda first: pl.BlockSpec((1, 1, s // TK, HEAD_DIM, TK), lambda i, g, j: (i, first + g, 0, 0, 0))
    return pl.pallas_call(
        _nsa_kernel,
        grid=(b, N_KV, s // TQ),
        in_specs=[q_spec(HEAD_DIM), q_spec(LANES),
                  pl.BlockSpec((1, 1, n_chunk, HEAD_DIM), lambda i, g, j: (i, g, 0, 0)),
                  pl.BlockSpec((1, 1, HEAD_DIM, n_chunk), lambda i, g, j: (i, N_KV + g, 0, 0)),
                  k_spec, vt_spec(0), k_spec, vt_spec(N_KV),
                  pl.BlockSpec((1, TQ, LANES), lambda i, g, j: (i, j, g))],
        out_specs=pl.BlockSpec((1, TQ, N_REP * HEAD_DIM), lambda i, g, j: (i, j, g)),
        out_shape=jax.ShapeDtypeStruct((b, s, BRANCH_W), F32),
        scratch_shapes=[pltpu.VMEM((N_REP, TQ, LANES), BF16), pltpu.VMEM((TQ // LANES, n_chunk, LANES), F32)],
        compiler_params=_params(3),
    )(q_all, q_rot, kc, kct, k_slc, vt, k_win, vt, gates)


def _swa_kernel(q_ref, k_ref, vt_ref, sink_ref, o_ref):
    sink = jnp.concatenate([sink_ref[0, r:r + 1, :] for r in range(N_REP)], axis=1) * LOG2E
    row = lax.broadcasted_iota(jnp.int32, (2 * SWA_TQ, SWA_TQ), 0)
    col = lax.broadcasted_iota(jnp.int32, (2 * SWA_TQ, SWA_TQ), 1)
    subs = range(SWA_ROWS // SWA_TQ)
    qts = [pl.program_id(2) * len(subs) + sub for sub in subs]
    kts = [jnp.maximum(qt - 1, 0) for qt in qts]
    ss = []
    for sub, qt, kt in zip(subs, qts, kts):
        q = q_ref[0, :, sub * SWA_TQ:(sub + 1) * SWA_TQ, :].reshape(N_REP * SWA_TQ, HEAD_DIM)
        key, tq = kt * SWA_TQ + row, qt * SWA_TQ + col
        bias = _heads_on_lanes(jnp.where((key <= tq) & (key > tq - SWA_WINDOW), 0.0, NEG))
        ss.append(_dot_t(k_ref[0, 0, pl.ds(pl.multiple_of(kt * SWA_TQ, SWA_TQ), 2 * SWA_TQ), :], q) + bias)
    ms = [jnp.maximum(jnp.max(s, axis=0, keepdims=True), sink) for s in ss]
    ps = [jnp.exp2(s - m).astype(BF16) for s, m in zip(ss, ms)]
    accs = [_dot(_with_ones(jnp.concatenate([vt_ref[0, 0, kt], vt_ref[0, 0, kt + 1]], axis=1)), p)
            for kt, p in zip(kts, ps)]
    for sub, acc, m in zip(subs, accs, ms):
        out = acc[:HEAD_DIM] * (1.0 / (acc[HEAD_DIM:HEAD_DIM + 1] + jnp.exp2(sink - m)))
        o_ref[0, sub * SWA_TQ:(sub + 1) * SWA_TQ, :] = _heads_to_rows(out)


def _swa(q_all, k_d, vt, l, sinks):
    b, _, s, _ = q_all.shape
    return pl.pallas_call(
        _swa_kernel,
        grid=(b, N_KV, s // SWA_ROWS),
        in_specs=[pl.BlockSpec((1, N_REP, SWA_ROWS, HEAD_DIM), lambda i, g, j: (i, N_KV + g, j, 0)),
                  pl.BlockSpec((1, 1, s, HEAD_DIM), lambda i, g, j: (i, g, 0, 0)),
                  pl.BlockSpec((1, 1, s // SWA_TQ, HEAD_DIM, SWA_TQ), lambda i, g, j: (i, g, 0, 0, 0)),
                  pl.BlockSpec((None, 1, N_REP, SWA_TQ), lambda i, g, j: (l, g, 0, 0))],
        out_specs=pl.BlockSpec((1, SWA_ROWS, N_REP * HEAD_DIM), lambda i, g, j: (i, j, g)),
        out_shape=jax.ShapeDtypeStruct((b, s, BRANCH_W), F32),
        compiler_params=_params(3),
    )(q_all, k_d, vt, sinks)


def _mix_kernel(x_ref, oa_ref, od_ref, glu_ref, glu_prev_ref, cin_ref, cin_prev_ref,
                cw_ref, cb_ref, lg_ref, lb_ref, pw_ref, ps_ref, gpre_ref, gpost_ref, wg_ref, wb_ref, wo_ref,
                o_ref, gext, gshift, cext):
    x = x_ref[0]
    hb = _rms(x, gpre_ref[...]).astype(BF16)

    def gated_up(n, branch):
        gate = jax.nn.sigmoid(_dot(hb, wg_ref[:, n * D_MODEL:(n + 1) * D_MODEL]))
        return gate * _dot(branch.astype(BF16), wb_ref[n])

    first = pl.program_id(1) == 0
    gext[:HALO] = jnp.where(first, 0.0, glu_prev_ref[0])
    gext[HALO:] = glu_ref[0]
    cext[:HALO] = jnp.where(first, 0.0, cin_prev_ref[0])
    cext[HALO:] = cin_ref[0]
    n_shift = HALO + MIX_TM - SUBLANES
    for ph in range(1, SUBLANES):
        gshift[ph - 1, :n_shift] = gext[pl.ds(ph, n_shift), :]

    mix = gated_up(0, oa_ref[0])

    acc = jnp.zeros((MIX_TM, BRANCH_W), F32) + cb_ref[...]
    for k in range(CONV_WIDTH):
        base, ph = divmod(HALO - (CONV_WIDTH - 1) + k, SUBLANES)
        rows = pl.ds(base * SUBLANES, MIX_TM)
        acc += cw_ref[k:k + 1, :] * (gext[rows, :] if ph == 0 else gshift[ph - 1, rows, :])
    mix += gated_up(N_BRANCH - 1, od_ref[0])
    mu = jnp.mean(acc, axis=-1, keepdims=True)
    cen = acc - mu
    y = cen * lax.rsqrt(jnp.mean(cen * cen, axis=-1, keepdims=True) + NORM_EPS) * lg_ref[...] + lb_ref[...]
    mix += gated_up(1, y * jax.nn.sigmoid(y))

    t = pl.program_id(1) * MIX_TM + lax.broadcasted_iota(jnp.int32, (MIX_TM, 1), 0)
    pooled = []
    for g, win in enumerate(POOL_WINDOWS):
        lanes = pl.ds(g * POOL_GROUP_CH, POOL_GROUP_CH)
        tot = cext[pl.ds(HALO, MIX_TM), lanes]
        for d in range(1, win):
            tot += cext[pl.ds(HALO - d, MIX_TM), lanes]
        mean_minus_token = tot / jnp.minimum(t + 1, win).astype(F32) - cext[pl.ds(HALO, MIX_TM), lanes]
        pooled.append(_dot(mean_minus_token.astype(BF16), pw_ref[g]))
    mix += gated_up(2, jnp.concatenate(pooled, axis=1) * ps_ref[...])

    o_ref[0] = x + _rms(_dot(mix.astype(BF16), wo_ref[...]), gpost_ref[...])


def _mix(x, oa, od, glu, cin, l, cw, cb, lg, lb, pw, ps, gpre, gpost, wg, wb, wo):
    b, s, d = x.shape
    xs = pl.BlockSpec((1, MIX_TM, d), lambda i, j: (i, j, 0))
    cur = pl.BlockSpec((1, MIX_TM, BRANCH_W), lambda i, j: (i, j, 0))
    prev = pl.BlockSpec((1, HALO, BRANCH_W), lambda i, j: (i, jnp.maximum(j * (MIX_TM // HALO) - 1, 0), 0))
    consts = (cw, cb, lg, lb, pw, ps, gpre, gpost, wg, wb, wo)
    return pl.pallas_call(
        _mix_kernel,
        grid=(b, s // MIX_TM),
        in_specs=[xs, cur, cur, cur, prev, cur, prev] + [_layer_spec(c, l) for c in consts],
        out_specs=xs,
        out_shape=jax.ShapeDtypeStruct(x.shape, F32),
        scratch_shapes=[pltpu.VMEM((HALO + MIX_TM, BRANCH_W), F32),
                        pltpu.VMEM((SUBLANES - 1, HALO + MIX_TM, BRANCH_W), F32),
                        pltpu.VMEM((HALO + MIX_TM, BRANCH_W), F32)],
        compiler_params=_params(2),
    )(x, oa, od, glu, glu, cin, cin, *consts)


def _ffn_kernel(x_ref, xprev_ref, gpre_ref, gpost_ref, wu_ref, cw_ref, cb_ref, wd_ref, o_ref):
    x = x_ref[0]
    first = pl.program_id(1) == 0
    xe = jnp.concatenate([jnp.where(first, 0.0, xprev_ref[0]), x], axis=0)
    hb = _rms(xe, gpre_ref[...]).astype(BF16)

    def up(c):
        return tuple(_dot(hb, wu_ref[:, col:col + FFN_CHUNK]) for col in (c * FFN_CHUNK, D_FF + c * FFN_CHUNK))

    def conv(u, col):
        out = cb_ref[:, col:col + FFN_CHUNK]
        for k in range(FFN_CONV_WIDTH):
            lo = FFN_HALO - (FFN_CONV_WIDTH - 1) + k
            out = out + cw_ref[k:k + 1, col:col + FFN_CHUNK] * u[lo:lo + FFN_TM]
        return out

    def down(c, act):
        return _dot(act, wd_ref[c * FFN_CHUNK:(c + 1) * FFN_CHUNK, :])

    n_chunks = D_FF // FFN_CHUNK
    f = jnp.zeros((FFN_TM, D_MODEL), F32)
    nxt, acts = up(0), []
    for c in range(n_chunks):
        ug, uv = nxt
        if c + 1 < n_chunks:
            nxt = up(c + 1)
        if len(acts) == DOWN_GROUP:
            f += _dot(jnp.concatenate(acts, axis=1), wd_ref[(c - DOWN_GROUP) * FFN_CHUNK:c * FFN_CHUNK, :])
            acts = []
        acts.append((jax.nn.gelu(conv(ug, c * FFN_CHUNK), approximate=True)
                     * conv(uv, D_FF + c * FFN_CHUNK)).astype(BF16))
    f += _dot(jnp.concatenate(acts, axis=1), wd_ref[(n_chunks - len(acts)) * FFN_CHUNK:, :])
    o_ref[0] = x + _rms(f, gpost_ref[...])


def _ffn(x, l, gpre, gpost, wu, cw, cb, wd):
    b, s, d = x.shape
    xs = pl.BlockSpec((1, FFN_TM, d), lambda i, j: (i, j, 0))
    prev = pl.BlockSpec((1, FFN_HALO, d), lambda i, j: (i, jnp.maximum(j * (FFN_TM // FFN_HALO) - 1, 0), 0))
    return pl.pallas_call(
        _ffn_kernel,
        grid=(b, s // FFN_TM),
        in_specs=[xs, prev] + [_layer_spec(c, l) for c in (gpre, gpost, wu, cw, cb, wd)],
        out_specs=xs,
        out_shape=jax.ShapeDtypeStruct(x.shape, F32),
        compiler_params=_params(2),
    )(x, x, gpre, gpost, wu, cw, cb, wd)


def _rope_tables(seq):
    inv = 1.0 / (ROPE_THETA ** (jnp.arange(0, HEAD_DIM, 2, dtype=F32) / HEAD_DIM))
    ang = jnp.arange(seq, dtype=F32)[:, None] * inv[None, :]
    cos, sin = jnp.cos(ang), jnp.sin(ang)
    reps = LANES // HEAD_DIM
    return jnp.tile(jnp.concatenate([cos, cos], -1), (1, reps)), jnp.tile(jnp.concatenate([-sin, sin], -1), (1, reps))


def kernel(x, norm_mix_pre, norm_mix_post, norm_ffn_pre, norm_ffn_post, w_in, nsa_cmp_pos, nsa_cmp_w1, nsa_cmp_w2, swa_sinks, conv_w, conv_b, conv_ln_g, conv_ln_b, pool_w, pool_scale, w_branch, w_gate, w_o, ffn_w_up, ffn_conv_w, ffn_conv_b, ffn_w_down):
    bsz, seq, d = x.shape
    assert d == D_MODEL and seq % TM == 0 and TQ == TK and TM % TK == 0 and SWA_TQ == SWA_WINDOW
    assert CMP_LEN == 2 * CMP_STRIDE and SEL_LEN % CMP_STRIDE == 0
    assert seq % FFN_TM == 0 and seq % SWA_ROWS == 0 and SWA_ROWS % SWA_TQ == 0 and D_FF % FFN_CHUNK == 0
    assert seq // SEL_LEN <= HEAD_DIM and (seq // SEL_LEN) % SUBLANES == 0 and seq // CMP_STRIDE <= LANES
    depth = w_in.shape[0]
    cos, sin = _rope_tables(seq)
    perm, v_cols = _in_col_permutation()
    rows = lambda v: v.reshape(depth, 1, -1)
    w_perm = _take_cols(w_in, perm).astype(BF16)
    wvt = jnp.swapaxes(_take_cols(w_in, v_cols).astype(BF16), 1, 2)
    cmp_pos = nsa_cmp_pos.reshape(depth, 2, 1, CMP_LEN * HEAD_DIM)
    cmp_w1, cmp_w2 = nsa_cmp_w1.astype(BF16), nsa_cmp_w2.astype(BF16)
    sinks = jnp.broadcast_to(swa_sinks.reshape(depth, N_KV, N_REP, 1), (depth, N_KV, N_REP, SWA_TQ))
    mix_params = (conv_w, rows(conv_b), rows(conv_ln_g), rows(conv_ln_b), pool_w.astype(BF16), rows(pool_scale),
                  rows(norm_mix_pre), rows(norm_mix_post), w_gate.astype(BF16), w_branch.astype(BF16), w_o.astype(BF16))
    ffn_params = (rows(norm_ffn_pre), rows(norm_ffn_post), ffn_w_up.astype(BF16), ffn_conv_w, rows(ffn_conv_b),
                  ffn_w_down.astype(BF16))
    for l in range(depth):
        q_all, q_rot, k_slc, k_win, k_d, chunks, vt_a, vt_d, glu, cin, gates = _proj(
            x, l, rows(norm_mix_pre), w_perm, wvt, cos, sin)
        kc, kct = _compress(chunks, l, cmp_pos, cmp_w1, cmp_w2, jnp.swapaxes(cmp_w2, 2, 3))
        o_a = _nsa(q_all, q_rot, kc, kct, k_slc, k_win, vt_a, gates)
        o_d = _swa(q_all, k_d, vt_d, l, sinks)
        x = _mix(x, o_a, o_d, glu, cin, l, *mix_params)
        x = _ffn(x, l, *ffn_params)
    return x
```

```python
import numpy as np
import jax
import jax.numpy as jnp
from jax import lax
from jax.experimental import pallas as pl
from jax.experimental.pallas import tpu as pltpu

F32 = jnp.float32
BF16 = jnp.bfloat16

D_MODEL = 1024
HEAD_DIM = 64
HALF = HEAD_DIM // 2
ROPE_THETA = 10000.0
NORM_EPS = 1e-6
NEG = -1e30
M_INIT = -1e29
BRANCH_W = D_MODEL // 2
N_BRANCH = 4
N_HEADS = BRANCH_W // HEAD_DIM
N_KV = 2
N_REP = N_HEADS // N_KV
CMP_LEN = 32
CMP_STRIDE = 16
CMP_HIDDEN = 256
SEL_LEN = 64
SEL_SHIFT = 6
SEL_TOPK = 16
PER_SEL = SEL_LEN // CMP_STRIDE
RANK_CHAINS = 4
NSA_WINDOW = 512
FORCE_BONUS = 1e3
SWA_WINDOW = 128
CONV_WIDTH = 31
POOL_WINDOWS = (2, 4, 8, 16)
POOL_GROUP_CH = BRANCH_W // len(POOL_WINDOWS)
D_FF = ((8 * D_MODEL // 3) + 127) // 128 * 128
FFN_CONV_WIDTH = 3
IN_SIZES = (BRANCH_W, 3 * 2 * N_KV * HEAD_DIM, 3 * N_HEADS, BRANCH_W, 2 * N_KV * HEAD_DIM, 2 * BRANCH_W, BRANCH_W)

LANES = 128
SUBLANES = 8
TM = 512
TQ = 256
TK = 256
QL = N_REP * TQ
SWA_TQ = 128
SWA_ROWS = 512
ONES_ROWS = 16
LOG2E = 1.4426950408889634
HALO = 32
FFN_HALO = 8
FFN_CHUNK = 256
DOWN_GROUP = 3
VMEM_LIMIT = 56 * 1024 * 1024

ROPE_COLS = 2 * BRANCH_W + 3 * LANES
COL_CMP_K = ROPE_COLS
COL_CMP_V = COL_CMP_K + LANES
COL_GATE = COL_CMP_V + LANES
COL_B_IN = COL_GATE + N_KV * LANES
COL_C_IN = COL_B_IN + 2 * BRANCH_W
N_COLS = COL_C_IN + BRANCH_W
MXU_COLS = 256
N_VT = 3 * N_KV


def _in_col_permutation():
    off = np.cumsum((0,) + IN_SIZES)
    a_q, a_kv, a_gate, d_q, d_kv, b_in, c_in = (np.arange(off[i], off[i + 1]) for i in range(7))
    seg = lambda br, kv: a_kv[(br * 2 + kv) * LANES:(br * 2 + kv + 1) * LANES]
    gate = np.full((N_KV, LANES), -1, np.int64)
    for g in range(N_KV):
        for br in range(3):
            for r in range(N_REP):
                gate[g, br * N_REP + r] = a_gate[br * N_HEADS + g * N_REP + r]
    cols = np.concatenate([a_q, d_q, seg(1, 0), seg(2, 0), d_kv[:LANES],
                           seg(0, 0), seg(0, 1), gate.reshape(-1), b_in, c_in])
    assert cols.shape[0] == N_COLS
    return cols, np.concatenate([seg(1, 1), seg(2, 1), d_kv[LANES:]])


def _take_cols(w, cols):
    runs = []
    for c in (int(c) for c in cols):
        if runs and ((c < 0 and runs[-1][0] < 0) or (c >= 0 and runs[-1][0] >= 0 and c == sum(runs[-1]))):
            runs[-1][1] += 1
        else:
            runs.append([c, 1])
    parts = [jnp.zeros(w.shape[:-1] + (n,), w.dtype) if c < 0 else w[..., c:c + n] for c, n in runs]
    return jnp.concatenate(parts, axis=-1)


def _layer_spec(stacked, l):
    nd = stacked.ndim
    return pl.BlockSpec((None,) + stacked.shape[1:], lambda *_: (l,) + (0,) * (nd - 1), pipeline_mode=pl.Buffered(1))


def _params(n_grid):
    return pltpu.CompilerParams(dimension_semantics=("parallel",) * n_grid, vmem_limit_bytes=VMEM_LIMIT)


def _rms(x, g):
    return x * lax.rsqrt(jnp.mean(x * x, axis=-1, keepdims=True) + NORM_EPS) * g


def _dot(a, b):
    return jnp.dot(a, b, preferred_element_type=F32)


def _dot_t(a, b):
    return lax.dot_general(a, b, (((1,), (1,)), ((), ())), preferred_element_type=F32)


def _proj_kernel(x_ref, g_ref, w_ref, wvt_ref, cos_ref, sin_ref,
                 q_ref, qra_ref, ks_ref, kw_ref, kd_ref, cmp_ref, vta_ref, vtd_ref, glu_ref, cin_ref, gate_ref,
                 chunk_ref):
    hb = _rms(x_ref[0], g_ref[...]).astype(BF16)
    cos, sin = cos_ref[...], sin_ref[...]
    first_half = (lax.broadcasted_iota(jnp.int32, (TM, LANES), 1) & (HEAD_DIM - 1)) < HALF

    def mm(col, width=LANES):
        return _dot(hb, w_ref[:, col:col + width])

    def rope(z):
        partner = jnp.where(first_half, pltpu.roll(z, LANES - HALF, 1), pltpu.roll(z, HALF, 1))
        return z * cos + partner * sin

    def put_heads(ref, first, z):
        ref[0, first] = z[:, :HEAD_DIM].astype(ref.dtype)
        ref[0, first + 1] = z[:, HEAD_DIM:].astype(ref.dtype)

    lane = lax.broadcasted_iota(jnp.int32, (TM, LANES), 1)

    def put_wide(ref, first, z, fill=0.0):
        ref[0, first] = jnp.where(lane < HEAD_DIM, z, fill).astype(BF16)
        ref[0, first + 1] = jnp.where(lane < HEAD_DIM, pltpu.roll(z, HEAD_DIM, 1), fill).astype(BF16)

    def put_chunks(first, z):
        chunk_ref[...] = z
        for tok in range(CMP_STRIDE):
            rows = chunk_ref[pl.ds(tok, TM // CMP_STRIDE, stride=CMP_STRIDE), :]
            for g in range(N_KV):
                cmp_ref[0, first + g, :, tok * HEAD_DIM:(tok + 1) * HEAD_DIM] = rows[:, g * HEAD_DIM:(g + 1) * HEAD_DIM]

    scale = HEAD_DIM ** -0.5 * LOG2E

    def nsa_q(c):
        def put(z):
            put_heads(q_ref, 2 * c, z * scale)
            put_wide(qra_ref, 2 * c, rope(z * scale))
        return put

    def gate(g):
        def put(z):
            gate_ref[0, :, g * LANES:(g + 1) * LANES] = jax.nn.sigmoid(z)
        return put

    segments = [nsa_q(c) for c in range(BRANCH_W // LANES)]
    segments += [lambda z, c=c: put_heads(q_ref, N_HEADS + 2 * c, rope(z * scale))
                 for c in range(BRANCH_W // LANES)]

    def slc_k(z):
        blk = (pl.program_id(1) * TM + lax.broadcasted_iota(jnp.int32, (TM, LANES), 0)) >> SEL_SHIFT
        put_wide(ks_ref, 0, rope(z), jnp.where(lane - HEAD_DIM == blk, 1.0, 0.0))

    segments += [slc_k,
                 lambda z: put_wide(kw_ref, 0, rope(z)),
                 lambda z: put_heads(kd_ref, 0, rope(z)),
                 lambda z: put_chunks(0, z),
                 lambda z: put_chunks(N_KV, z)]
    segments += [gate(g) for g in range(N_KV)]
    per_dot = MXU_COLS // LANES
    for first in range(0, len(segments), per_dot):
        group = segments[first:first + per_dot]
        z = mm(first * LANES, len(group) * LANES)
        for i, put in enumerate(group):
            put(z[:, i * LANES:(i + 1) * LANES])
    vt = _dot_t(wvt_ref[...], hb).astype(BF16)
    for n in range(2 * N_KV):
        for c in range(TM // TK):
            vta_ref[0, n, c] = vt[n * HEAD_DIM:(n + 1) * HEAD_DIM, c * TK:(c + 1) * TK]
    for n in range(N_KV):
        rows = slice((2 * N_KV + n) * HEAD_DIM, (2 * N_KV + n + 1) * HEAD_DIM)
        for c in range(TM // SWA_TQ):
            vtd_ref[0, n, c] = vt[rows, c * SWA_TQ:(c + 1) * SWA_TQ]
    glu_ref[0] = mm(COL_B_IN, BRANCH_W) * jax.nn.sigmoid(mm(COL_B_IN + BRANCH_W, BRANCH_W))
    cin_ref[0] = mm(COL_C_IN, BRANCH_W)


def _proj(x, l, g, w, wvt, cos, sin):
    b, s, d = x.shape
    heads = lambda n, dt: (jax.ShapeDtypeStruct((b, n, s, HEAD_DIM), dt),
                           pl.BlockSpec((1, n, TM, HEAD_DIM), lambda i, j: (i, 0, j, 0)))
    rows = lambda n: (jax.ShapeDtypeStruct((b, s, n), F32), pl.BlockSpec((1, TM, n), lambda i, j: (i, j, 0)))
    vts = lambda n, t: (jax.ShapeDtypeStruct((b, n, s // t, HEAD_DIM, t), BF16),
                        pl.BlockSpec((1, n, TM // t, HEAD_DIM, t), lambda i, j: (i, 0, j, 0, 0)))
    wide = lambda n: (jax.ShapeDtypeStruct((b, n, s, LANES), BF16),
                      pl.BlockSpec((1, n, TM, LANES), lambda i, j: (i, 0, j, 0)))
    per_row = CMP_STRIDE * HEAD_DIM
    chunks = (jax.ShapeDtypeStruct((b, 2 * N_KV, s // CMP_STRIDE, per_row), F32),
              pl.BlockSpec((1, 2 * N_KV, TM // CMP_STRIDE, per_row), lambda i, j: (i, 0, j, 0)))
    outs = [heads(2 * N_HEADS, BF16), wide(N_HEADS), wide(N_KV), wide(N_KV), heads(N_KV, BF16), chunks,
            vts(2 * N_KV, TK), vts(N_KV, SWA_TQ), rows(BRANCH_W), rows(BRANCH_W), rows(N_KV * LANES)]
    return pl.pallas_call(
        _proj_kernel,
        grid=(b, s // TM),
        in_specs=[pl.BlockSpec((1, TM, d), lambda i, j: (i, j, 0)),
                  _layer_spec(g, l), _layer_spec(w, l), _layer_spec(wvt, l),
                  pl.BlockSpec((TM, LANES), lambda i, j: (j, 0)),
                  pl.BlockSpec((TM, LANES), lambda i, j: (j, 0))],
        out_specs=[o[1] for o in outs],
        out_shape=[o[0] for o in outs],
        scratch_shapes=[pltpu.VMEM((TM, LANES), F32)],
        compiler_params=_params(2),
    )(x, g, w, wvt, cos, sin)


def _compress_kernel(c_ref, pos_ref, w1_ref, w2_ref, w2t_ref, o_ref, ot_ref):
    c = c_ref[0, 0]
    n_chunk, half = c.shape
    pos = pos_ref[0]
    top = _dot((c + pos[:, :half]).astype(BF16), w1_ref[0, :half])
    bot = _dot((c + pos[:, half:]).astype(BF16), w1_ref[0, half:])
    hid = top + pltpu.roll(bot, n_chunk - 1, 0)
    act = jax.nn.gelu(hid, approximate=True).astype(BF16)
    out = _dot(act, w2_ref[0])
    row = lax.broadcasted_iota(jnp.int32, out.shape, 0)
    o_ref[0, 0] = jnp.where(row < n_chunk - 1, out, 0.0)
    out_t = _dot_t(w2t_ref[0], act)
    col = lax.broadcasted_iota(jnp.int32, out_t.shape, 1)
    ot_ref[0, 0] = jnp.where(col < n_chunk - 1, out_t, 0.0)


def _compress(chunks, l, pos, w1, w2, w2t):
    b, n, n_chunk, width = chunks.shape
    kv = lambda *dims: pl.BlockSpec((None, 1) + dims, lambda i, j: (l, j // N_KV, 0, 0))
    return pl.pallas_call(
        _compress_kernel,
        grid=(b, n),
        in_specs=[pl.BlockSpec((1, 1, n_chunk, width), lambda i, j: (i, j, 0, 0)),
                  kv(1, 2 * width), kv(2 * width, CMP_HIDDEN), kv(CMP_HIDDEN, HEAD_DIM), kv(HEAD_DIM, CMP_HIDDEN)],
        out_specs=[pl.BlockSpec((1, 1, n_chunk, HEAD_DIM), lambda i, j: (i, j, 0, 0)),
                   pl.BlockSpec((1, 1, HEAD_DIM, n_chunk), lambda i, j: (i, j, 0, 0))],
        out_shape=[jax.ShapeDtypeStruct((b, n, n_chunk, HEAD_DIM), F32),
                   jax.ShapeDtypeStruct((b, n, HEAD_DIM, n_chunk), F32)],
        compiler_params=_params(2),
    )(chunks, pos, w1, w2, w2t)


def _heads_on_lanes(a):
    return jnp.concatenate([a] * N_REP, axis=1)


def _with_ones(vt):
    return jnp.concatenate([vt, jnp.ones((ONES_ROWS, vt.shape[1]), vt.dtype)], axis=0)


def _scores(k_ref, q_refs, kt):
    k = k_ref[0, 0, pl.ds(pl.multiple_of(kt * TK, TK), TK), :]
    return tuple(_dot_t(k, q[...]) for q in q_refs)


def _absorb(k_ref, vt_ref, q_refs, tiles, state):
    ss = [_scores(k_ref, q_refs, kt) for kt, _ in tiles]
    ss = [s if bias is None else [x + bias for x in s] for s, (_, bias) in zip(ss, tiles)]
    ms = []
    for r, (m, _) in enumerate(state):
        for s in ss:
            m = jnp.maximum(m, jnp.max(s[r], axis=0, keepdims=True))
        ms.append(m)
    ps = [jnp.concatenate([jnp.exp2(s[r] - ms[r]).astype(BF16) for s in ss], axis=0) for r in range(N_REP)]
    vt1 = jnp.concatenate([_with_ones(vt_ref[0, 0, kt]) for kt, _ in tiles], axis=1)
    return tuple((m_new, jnp.exp2(m - m_new) * acc + _dot(vt1, p)) for p, m_new, (m, acc) in zip(ps, ms, state))


def _online_init():
    return tuple((jnp.full((1, TQ), M_INIT, F32), jnp.zeros((HEAD_DIM + ONES_ROWS, TQ), F32)) for _ in range(N_REP))


def _normed(state):
    return [acc[:HEAD_DIM] * (1.0 / acc[HEAD_DIM:HEAD_DIM + 1]) for _, acc in state]


def _band_attention(q_refs, k_ref, vt_ref, qi, window):
    tq = qi * TQ + lax.broadcasted_iota(jnp.int32, (TK, TQ), 1)
    row = lax.broadcasted_iota(jnp.int32, (TK, TQ), 0)
    tiles = []
    for d in range(window // TK + 1):
        key = (qi - d) * TK + row
        tiles.append((jnp.maximum(qi - d, 0), jnp.where((key <= tq) & (key > tq - window) & (key >= 0), 0.0, NEG)))
    return _absorb(k_ref, vt_ref, q_refs, tiles, _online_init())


def _heads_to_rows(o_t):
    tq = o_t.shape[1] // N_REP
    return jnp.concatenate([o_t[:, r * tq:(r + 1) * tq] for r in range(N_REP)], axis=0).T


def _nsa_kernel(q_ref, qr_ref, kc_ref, vct_ref, ks_ref, vst_ref, kw_ref, vwt_ref, gate_ref, o_ref,
                qaug_ref, psum_ref):
    qi = pl.program_id(2)
    q = q_ref[0].reshape(QL, HEAD_DIM)
    n_chunk = kc_ref.shape[2]
    n_slc = ks_ref.shape[2] // SEL_LEN

    win = _band_attention([qr_ref.at[0, r] for r in range(N_REP)], kw_ref, vwt_ref, qi, NSA_WINDOW)

    blk = lax.broadcasted_iota(jnp.int32, (n_chunk, TQ), 0)
    tqc = qi * TQ + lax.broadcasted_iota(jnp.int32, (n_chunk, TQ), 1)
    vis = (blk * CMP_STRIDE + CMP_LEN - 1 <= tqc) & (blk < n_chunk - 1)
    s = _dot_t(kc_ref[0, 0].astype(BF16), q) + _heads_on_lanes(jnp.where(vis, 0.0, NEG))
    e = jnp.exp2(s - jnp.maximum(jnp.max(s, axis=0, keepdims=True), M_INIT))
    p = e * (1.0 / jnp.maximum(jnp.sum(e, axis=0, keepdims=True), 1e-30))
    o_cmp = _dot(vct_ref[0, 0].astype(BF16), p.astype(BF16))

    p_sum = p[:, :TQ]
    for r in range(1, N_REP):
        p_sum = p_sum + p[:, r * TQ:(r + 1) * TQ]
    halves = range(TQ // LANES)
    for h in halves:
        psum_ref[h] = p_sum[:, h * LANES:(h + 1) * LANES]
    inside = [jnp.concatenate([psum_ref[h, pl.ds(k, n_slc, stride=PER_SEL), :] for h in halves], axis=1)
              for k in range(PER_SEL)]
    sb = lax.broadcasted_iota(jnp.int32, (n_slc, TQ), 0)
    imp = jnp.where(sb == 0, 0.0, pltpu.roll(inside[-1], 1, 0))
    for part in inside:
        imp = imp + part
    cur = (qi * TQ + lax.broadcasted_iota(jnp.int32, (n_slc, TQ), 1)) >> SEL_SHIFT
    forced = (sb == 0) | (sb == cur) | (sb == cur - 1)
    score = jnp.where(sb <= cur, imp + jnp.where(forced, FORCE_BONUS, 0.0), -1.0)
    ranks = [jnp.zeros((n_slc, TQ), F32) for _ in range(RANK_CHAINS)]
    for i in range(n_slc):
        si = score[i:i + 1, :]
        ranks[i % RANK_CHAINS] += jnp.where((si > score) | ((si == score) & (sb > i)), 1.0, 0.0)
    sel_bias = jnp.where(sum(ranks) < min(SEL_TOPK, n_slc), 0.0, NEG)

    zeros = lambda n: jnp.zeros((n, TQ), F32)
    bias_t = jnp.concatenate([zeros(HEAD_DIM), sel_bias, zeros(LANES - HEAD_DIM - n_slc)], axis=0).T.astype(BF16)
    for r in range(N_REP):
        qaug_ref[r] = qr_ref[0, r] + bias_t

    qaug = [qaug_ref.at[r] for r in range(N_REP)]

    def slc_pair(i, state):
        return _absorb(ks_ref, vst_ref, qaug, [(2 * i, None), (2 * i + 1, None)], state)

    slc = lax.fori_loop(0, qi // 2, slc_pair, _online_init())
    tq = lax.broadcasted_iota(jnp.int32, (TK, TQ), 1)
    row = lax.broadcasted_iota(jnp.int32, (TK, TQ), 0)
    diagonal = (qi, jnp.where(row <= tq, 0.0, NEG))
    slc = lax.cond(qi % 2 == 1,
                   lambda state: _absorb(ks_ref, vst_ref, qaug, [(qi - 1, None), diagonal], state),
                   lambda state: _absorb(ks_ref, vst_ref, qaug, [diagonal], state), slc)

    gate_t = gate_ref[0].T
    heads = []
    for r, (o_slc, o_win) in enumerate(zip(_normed(slc), _normed(win))):
        g = lambda br: gate_t[br * N_REP + r:br * N_REP + r + 1, :]
        heads.append(g(0) * o_cmp[:, r * TQ:(r + 1) * TQ] + g(1) * o_slc + g(2) * o_win)
    o_ref[0] = jnp.concatenate(heads, axis=0).T


def _nsa(q_all, q_rot, kc, kct, k_slc, k_win, vt, gates):
    b, _, s, _ = q_all.shape
    n_chunk = kc.shape[2]
    q_spec = lambda width: pl.BlockSpec((1, N_REP, TQ, width), lambda i, g, j: (i, g, j, 0))
    k_spec = pl.BlockSpec((1, 1, s, LANES), lambda i, g, j: (i, g, 0, 0))
    vt_spec = lambda first: pl.BlockSpec((1, 1, s // TK, HEAD_DIM, TK), lambda i, g, j: (i, first + g, 0, 0, 0))
    return pl.pallas_call(
        _nsa_kernel,
        grid=(b, N_KV, s // TQ),
        in_specs=[q_spec(HEAD_DIM), q_spec(LANES),
                  pl.BlockSpec((1, 1, n_chunk, HEAD_DIM), lambda i, g, j: (i, g, 0, 0)),
                  pl.BlockSpec((1, 1, HEAD_DIM, n_chunk), lambda i, g, j: (i, N_KV + g, 0, 0)),
                  k_spec, vt_spec(0), k_spec, vt_spec(N_KV),
                  pl.BlockSpec((1, TQ, LANES), lambda i, g, j: (i, j, g))],
        out_specs=pl.BlockSpec((1, TQ, N_REP * HEAD_DIM), lambda i, g, j: (i, j, g)),
        out_shape=jax.ShapeDtypeStruct((b, s, BRANCH_W), F32),
        scratch_shapes=[pltpu.VMEM((N_REP, TQ, LANES), BF16), pltpu.VMEM((TQ // LANES, n_chunk, LANES), F32)],
        compiler_params=_params(3),
    )(q_all, q_rot, kc, kct, k_slc, vt, k_win, vt, gates)


def _swa_kernel(q_ref, k_ref, vt_ref, sink_ref, o_ref):
    sink = jnp.concatenate([sink_ref[0, r:r + 1, :] for r in range(N_REP)], axis=1) * LOG2E
    row = lax.broadcasted_iota(jnp.int32, (2 * SWA_TQ, SWA_TQ), 0)
    col = lax.broadcasted_iota(jnp.int32, (2 * SWA_TQ, SWA_TQ), 1)
    subs = range(SWA_ROWS // SWA_TQ)
    qts = [pl.program_id(2) * len(subs) + sub for sub in subs]
    kts = [jnp.maximum(qt - 1, 0) for qt in qts]
    ss = []
    for sub, qt, kt in zip(subs, qts, kts):
        q = q_ref[0, :, sub * SWA_TQ:(sub + 1) * SWA_TQ, :].reshape(N_REP * SWA_TQ, HEAD_DIM)
        key, tq = kt * SWA_TQ + row, qt * SWA_TQ + col
        bias = _heads_on_lanes(jnp.where((key <= tq) & (key > tq - SWA_WINDOW), 0.0, NEG))
        ss.append(_dot_t(k_ref[0, 0, pl.ds(pl.multiple_of(kt * SWA_TQ, SWA_TQ), 2 * SWA_TQ), :], q) + bias)
    ms = [jnp.maximum(jnp.max(s, axis=0, keepdims=True), sink) for s in ss]
    ps = [jnp.exp2(s - m).astype(BF16) for s, m in zip(ss, ms)]
    accs = [_dot(_with_ones(jnp.concatenate([vt_ref[0, 0, kt], vt_ref[0, 0, kt + 1]], axis=1)), p)
            for kt, p in zip(kts, ps)]
    for sub, acc, m in zip(subs, accs, ms):
        out = acc[:HEAD_DIM] * (1.0 / (acc[HEAD_DIM:HEAD_DIM + 1] + jnp.exp2(sink - m)))
        o_ref[0, sub * SWA_TQ:(sub + 1) * SWA_TQ, :] = _heads_to_rows(out)


def _swa(q_all, k_d, vt, l, sinks):
    b, _, s, _ = q_all.shape
    return pl.pallas_call(
        _swa_kernel,
        grid=(b, N_KV, s // SWA_ROWS),
        in_specs=[pl.BlockSpec((1, N_REP, SWA_ROWS, HEAD_DIM), lambda i, g, j: (i, N_KV + g, j, 0)),
                  pl.BlockSpec((1, 1, s, HEAD_DIM), lambda i, g, j: (i, g, 0, 0)),
                  pl.BlockSpec((1, 1, s // SWA_TQ, HEAD_DIM, SWA_TQ), lambda i, g, j: (i, g, 0, 0, 0)),
                  pl.BlockSpec((None, 1, N_REP, SWA_TQ), lambda i, g, j: (l, g, 0, 0))],
        out_specs=pl.BlockSpec((1, SWA_ROWS, N_REP * HEAD_DIM), lambda i, g, j: (i, j, g)),
        out_shape=jax.ShapeDtypeStruct((b, s, BRANCH_W), F32),
        compiler_params=_params(3),
    )(q_all, k_d, vt, sinks)


def _mix_kernel(x_ref, oa_ref, od_ref, glu_ref, glu_prev_ref, cin_ref, cin_prev_ref,
                cw_ref, cb_ref, lg_ref, lb_ref, pw_ref, ps_ref, gpre_ref, gpost_ref, wg_ref, wb_ref, wo_ref,
                o_ref, gext, gshift, cext):
    x = x_ref[0]
    hb = _rms(x, gpre_ref[...]).astype(BF16)

    def gated_up(n, branch):
        gate = jax.nn.sigmoid(_dot(hb, wg_ref[:, n * D_MODEL:(n + 1) * D_MODEL]))
        return gate * _dot(branch.astype(BF16), wb_ref[n])

    first = pl.program_id(1) == 0
    gext[:HALO] = jnp.where(first, 0.0, glu_prev_ref[0])
    gext[HALO:] = glu_ref[0]
    cext[:HALO] = jnp.where(first, 0.0, cin_prev_ref[0])
    cext[HALO:] = cin_ref[0]
    n_shift = HALO + TM - SUBLANES
    for ph in range(1, SUBLANES):
        gshift[ph - 1, :n_shift] = gext[pl.ds(ph, n_shift), :]

    mix = gated_up(0, oa_ref[0])

    acc = jnp.zeros((TM, BRANCH_W), F32) + cb_ref[...]
    for k in range(CONV_WIDTH):
        base, ph = divmod(HALO - (CONV_WIDTH - 1) + k, SUBLANES)
        rows = pl.ds(base * SUBLANES, TM)
        acc += cw_ref[k:k + 1, :] * (gext[rows, :] if ph == 0 else gshift[ph - 1, rows, :])
    mix += gated_up(N_BRANCH - 1, od_ref[0])
    mu = jnp.mean(acc, axis=-1, keepdims=True)
    cen = acc - mu
    y = cen * lax.rsqrt(jnp.mean(cen * cen, axis=-1, keepdims=True) + NORM_EPS) * lg_ref[...] + lb_ref[...]
    mix += gated_up(1, y * jax.nn.sigmoid(y))

    t = pl.program_id(1) * TM + lax.broadcasted_iota(jnp.int32, (TM, 1), 0)
    pooled = []
    for g, win in enumerate(POOL_WINDOWS):
        lanes = pl.ds(g * POOL_GROUP_CH, POOL_GROUP_CH)
        tot = cext[pl.ds(HALO, TM), lanes]
        for d in range(1, win):
            tot += cext[pl.ds(HALO - d, TM), lanes]
        mean_minus_token = tot / jnp.minimum(t + 1, win).astype(F32) - cext[pl.ds(HALO, TM), lanes]
        pooled.append(_dot(mean_minus_token.astype(BF16), pw_ref[g]))
    mix += gated_up(2, jnp.concatenate(pooled, axis=1) * ps_ref[...])

    o_ref[0] = x + _rms(_dot(mix.astype(BF16), wo_ref[...]), gpost_ref[...])


def _mix(x, oa, od, glu, cin, l, cw, cb, lg, lb, pw, ps, gpre, gpost, wg, wb, wo):
    b, s, d = x.shape
    xs = pl.BlockSpec((1, TM, d), lambda i, j: (i, j, 0))
    cur = pl.BlockSpec((1, TM, BRANCH_W), lambda i, j: (i, j, 0))
    prev = pl.BlockSpec((1, HALO, BRANCH_W), lambda i, j: (i, jnp.maximum(j * (TM // HALO) - 1, 0), 0))
    consts = (cw, cb, lg, lb, pw, ps, gpre, gpost, wg, wb, wo)
    return pl.pallas_call(
        _mix_kernel,
        grid=(b, s // TM),
        in_specs=[xs, cur, cur, cur, prev, cur, prev] + [_layer_spec(c, l) for c in consts],
        out_specs=xs,
        out_shape=jax.ShapeDtypeStruct(x.shape, F32),
        scratch_shapes=[pltpu.VMEM((HALO + TM, BRANCH_W), F32),
                        pltpu.VMEM((SUBLANES - 1, HALO + TM, BRANCH_W), F32),
                        pltpu.VMEM((HALO + TM, BRANCH_W), F32)],
        compiler_params=_params(2),
    )(x, oa, od, glu, glu, cin, cin, *consts)


def _ffn_kernel(x_ref, xprev_ref, gpre_ref, gpost_ref, wu_ref, cw_ref, cb_ref, wd_ref, o_ref):
    x = x_ref[0]
    first = pl.program_id(1) == 0
    xe = jnp.concatenate([jnp.where(first, 0.0, xprev_ref[0]), x], axis=0)
    hb = _rms(xe, gpre_ref[...]).astype(BF16)

    def up(c):
        return tuple(_dot(hb, wu_ref[:, col:col + FFN_CHUNK]) for col in (c * FFN_CHUNK, D_FF + c * FFN_CHUNK))

    def conv(u, col):
        out = cb_ref[:, col:col + FFN_CHUNK]
        for k in range(FFN_CONV_WIDTH):
            lo = FFN_HALO - (FFN_CONV_WIDTH - 1) + k
            out = out + cw_ref[k:k + 1, col:col + FFN_CHUNK] * u[lo:lo + TM]
        return out

    def down(c, act):
        return _dot(act, wd_ref[c * FFN_CHUNK:(c + 1) * FFN_CHUNK, :])

    n_chunks = D_FF // FFN_CHUNK
    f = jnp.zeros((TM, D_MODEL), F32)
    nxt, acts = up(0), []
    for c in range(n_chunks):
        ug, uv = nxt
        if c + 1 < n_chunks:
            nxt = up(c + 1)
        if len(acts) == DOWN_GROUP:
            f += _dot(jnp.concatenate(acts, axis=1), wd_ref[(c - DOWN_GROUP) * FFN_CHUNK:c * FFN_CHUNK, :])
            acts = []
        acts.append((jax.nn.gelu(conv(ug, c * FFN_CHUNK), approximate=True)
                     * conv(uv, D_FF + c * FFN_CHUNK)).astype(BF16))
    f += _dot(jnp.concatenate(acts, axis=1), wd_ref[(n_chunks - len(acts)) * FFN_CHUNK:, :])
    o_ref[0] = x + _rms(f, gpost_ref[...])


def _ffn(x, l, gpre, gpost, wu, cw, cb, wd):
    b, s, d = x.shape
    xs = pl.BlockSpec((1, TM, d), lambda i, j: (i, j, 0))
    prev = pl.BlockSpec((1, FFN_HALO, d), lambda i, j: (i, jnp.maximum(j * (TM // FFN_HALO) - 1, 0), 0))
    return pl.pallas_call(
        _ffn_kernel,
        grid=(b, s // TM),
        in_specs=[xs, prev] + [_layer_spec(c, l) for c in (gpre, gpost, wu, cw, cb, wd)],
        out_specs=xs,
        out_shape=jax.ShapeDtypeStruct(x.shape, F32),
        compiler_params=_params(2),
    )(x, x, gpre, gpost, wu, cw, cb, wd)


def _rope_tables(seq):
    inv = 1.0 / (ROPE_THETA ** (jnp.arange(0, HEAD_DIM, 2, dtype=F32) / HEAD_DIM))
    ang = jnp.arange(seq, dtype=F32)[:, None] * inv[None, :]
    cos, sin = jnp.cos(ang), jnp.sin(ang)
    reps = LANES // HEAD_DIM
    return jnp.tile(jnp.concatenate([cos, cos], -1), (1, reps)), jnp.tile(jnp.concatenate([-sin, sin], -1), (1, reps))


def kernel(x, norm_mix_pre, norm_mix_post, norm_ffn_pre, norm_ffn_post, w_in, nsa_cmp_pos, nsa_cmp_w1, nsa_cmp_w2, swa_sinks, conv_w, conv_b, conv_ln_g, conv_ln_b, pool_w, pool_scale, w_branch, w_gate, w_o, ffn_w_up, ffn_conv_w, ffn_conv_b, ffn_w_down):
    bsz, seq, d = x.shape
    assert d == D_MODEL and seq % TM == 0 and TQ == TK and TM % TK == 0 and SWA_TQ == SWA_WINDOW
    assert CMP_LEN == 2 * CMP_STRIDE and SEL_LEN % CMP_STRIDE == 0
    assert seq % SWA_ROWS == 0 and SWA_ROWS % SWA_TQ == 0 and D_FF % FFN_CHUNK == 0
    assert seq // SEL_LEN <= HEAD_DIM and (seq // SEL_LEN) % SUBLANES == 0 and seq // CMP_STRIDE <= LANES
    depth = w_in.shape[0]
    cos, sin = _rope_tables(seq)
    perm, v_cols = _in_col_permutation()
    rows = lambda v: v.reshape(depth, 1, -1)
    w_perm = _take_cols(w_in, perm).astype(BF16)
    wvt = jnp.swapaxes(_take_cols(w_in, v_cols).astype(BF16), 1, 2)
    cmp_pos = nsa_cmp_pos.reshape(depth, 2, 1, CMP_LEN * HEAD_DIM)
    cmp_w1, cmp_w2 = nsa_cmp_w1.astype(BF16), nsa_cmp_w2.astype(BF16)
    sinks = jnp.broadcast_to(swa_sinks.reshape(depth, N_KV, N_REP, 1), (depth, N_KV, N_REP, SWA_TQ))
    mix_params = (conv_w, rows(conv_b), rows(conv_ln_g), rows(conv_ln_b), pool_w.astype(BF16), rows(pool_scale),
                  rows(norm_mix_pre), rows(norm_mix_post), w_gate.astype(BF16), w_branch.astype(BF16), w_o.astype(BF16))
    ffn_params = (rows(norm_ffn_pre), rows(norm_ffn_post), ffn_w_up.astype(BF16), ffn_conv_w, rows(ffn_conv_b),
                  ffn_w_down.astype(BF16))
    for l in range(depth):
        q_all, q_rot, k_slc, k_win, k_d, chunks, vt_a, vt_d, glu, cin, gates = _proj(
            x, l, rows(norm_mix_pre), w_perm, wvt, cos, sin)
        kc, kct = _compress(chunks, l, cmp_pos, cmp_w1, cmp_w2, jnp.swapaxes(cmp_w2, 2, 3))
        o_a = _nsa(q_all, q_rot, kc, kct, k_slc, k_win, vt_a, gates)
        o_d = _swa(q_all, k_d, vt_d, l, sinks)
        x = _mix(x, o_a, o_d, glu, cin, l, *mix_params)
        x = _ffn(x, l, *ffn_params)
    return x
```
